```python
import math
import jax
import jax.numpy as jnp
from jax import lax
import numpy as np

D_MODEL = 1024
BATCH = 8
SEQ = 4096
DEPTH = 2
DEC_BATCH = 32
DEC_SEQ = 8
PAST_LEN = 16384
PAGE_SIZE = 128

HEAD_DIM = 64
MOBA_HEADS = 6
DIFF_HEADS = 4
GMLP_GROUPS = 6
MOBA_W = MOBA_HEADS * HEAD_DIM
DIFF_W = DIFF_HEADS * HEAD_DIM
GMLP_W = GMLP_GROUPS * HEAD_DIM
MIX_W = MOBA_W + DIFF_W + GMLP_W
KV_HEADS = MOBA_HEADS + DIFF_HEADS
DIFF_QK = HEAD_DIM // 2
PROJ_SIZES = (MOBA_W, MOBA_W, MOBA_W, DIFF_W, DIFF_W, DIFF_W, GMLP_W, GMLP_W)
PROJ_W = 2688
MOBA_BLOCK = 256
MOBA_TOPK = 3
Q_BLOCK = 128
GMLP_CHUNK = 128
NUM_BUCKETS = 32
MAX_EXACT = 16
MAX_DISTANCE = 2048
D_FF = 2816
N_EXPERTS = 8
TOP_K = 2
D_PLE = 256
RMS_EPS = 1e-6
F32 = jnp.float32
NEG_INF = float('-inf')

kernel_name = 'hybrid_moba_diffattn_gmlp_decode_step'


def rmsnorm(x, g):
    xf = x.astype(F32)
    y = xf * lax.rsqrt(jnp.mean(xf * xf, axis=-1, keepdims=True) + RMS_EPS)
    return (y * g.astype(F32)).astype(x.dtype)


def lambda_init(i):
    return 0.8 - 0.6 * math.exp(-0.3 * i)


def t5_bucket(dist):
    n = jnp.maximum(dist, 0)
    nf = jnp.maximum(n, MAX_EXACT).astype(F32)
    large = MAX_EXACT + (jnp.log(nf / MAX_EXACT) / math.log(MAX_DISTANCE / MAX_EXACT)
                         * (NUM_BUCKETS - MAX_EXACT)).astype(jnp.int32)
    return jnp.where(n < MAX_EXACT, n, jnp.minimum(large, NUM_BUCKETS - 1))


def moba_core(q, q_pos, kb, vb, kmean, k_own, v_own, own_pos, j0, n_sel, bias_t):
    B, Q, H, dh = q.shape
    scale = dh ** -0.5
    l_own = jnp.einsum('bqhd,bkhd->bhqk', q, k_own, preferred_element_type=F32) * scale
    l_own = l_own + bias_t[:, t5_bucket(q_pos[:, None] - own_pos[None, :])][None]
    l_own = jnp.where(own_pos[None, :] <= q_pos[:, None], l_own, NEG_INF)
    if n_sel == 0:
        p = jax.nn.softmax(l_own, axis=-1).astype(v_own.dtype)
        return jnp.einsum('bhqk,bkhd->bqhd', p, v_own)
    nb, blk = kb.shape[1], kb.shape[2]
    gate = jnp.einsum('bqhd,bnhd->bhqn', q.astype(F32), kmean)
    gate = jnp.where(jnp.arange(nb) < j0, gate, NEG_INF)
    _, sel = lax.top_k(gate, n_sel)
    bi = jnp.arange(B)[:, None, None, None]
    hi = jnp.arange(H)[None, :, None, None]
    k_sel = kb[bi, sel, :, hi]
    v_sel = vb[bi, sel, :, hi]
    sel_pos = sel[..., None] * blk + jnp.arange(blk)
    l_sel = jnp.einsum('bqhd,bhqnkd->bhqnk', q, k_sel, preferred_element_type=F32) * scale
    l_sel = l_sel + bias_t[hi[..., None], t5_bucket(q_pos[:, None, None] - sel_pos)]
    l_sel = jnp.where((sel < j0)[..., None], l_sel, NEG_INF).reshape(B, H, Q, n_sel * blk)
    p = jax.nn.softmax(jnp.concatenate([l_sel, l_own], axis=-1), axis=-1).astype(v_own.dtype)
    p_sel = p[..., : n_sel * blk].reshape(B, H, Q, n_sel, blk)
    out = jnp.einsum('bhqnk,bhqnkd->bqhd', p_sel, v_sel)
    return out + jnp.einsum('bhqk,bkhd->bqhd', p[..., n_sel * blk:], v_own)


def moba_prompt(q, k, v, bias_t):
    B, S, H, dh = q.shape
    nb = -(-S // MOBA_BLOCK)
    pad = ((0, 0), (0, nb * MOBA_BLOCK - S), (0, 0), (0, 0))
    kp, vp = jnp.pad(k, pad), jnp.pad(v, pad)
    kb = kp.reshape(B, nb, MOBA_BLOCK, H, dh)
    vb = vp.reshape(B, nb, MOBA_BLOCK, H, dh)
    kmean = jnp.mean(kb.astype(F32), axis=2)
    n_sel = min(MOBA_TOPK, nb - 1)
    nq = S // Q_BLOCK
    q_blocks = jnp.swapaxes(q.reshape(B, nq, Q_BLOCK, H, dh), 0, 1)

    def step(args):
        qi, i = args
        start = i * Q_BLOCK
        j0 = start // MOBA_BLOCK
        k_own = lax.dynamic_slice_in_dim(kp, j0 * MOBA_BLOCK, MOBA_BLOCK, axis=1)
        v_own = lax.dynamic_slice_in_dim(vp, j0 * MOBA_BLOCK, MOBA_BLOCK, axis=1)
        own_pos = j0 * MOBA_BLOCK + jnp.arange(MOBA_BLOCK)
        q_pos = start + jnp.arange(Q_BLOCK)
        return moba_core(qi, q_pos, kb, vb, kmean, k_own, v_own, own_pos, j0, n_sel, bias_t)

    out = lax.map(step, (q_blocks, jnp.arange(nq, dtype=jnp.int32)))
    return jnp.swapaxes(out, 0, 1).reshape(B, S, H, dh)


def moba_sample(q, k_all, v_all, bias_t):
    B, L, H, dh = k_all.shape
    T = q.shape[1]
    start = L - T
    j0 = start // MOBA_BLOCK
    n_sel = min(MOBA_TOPK, j0)
    kb = k_all[:, : j0 * MOBA_BLOCK].reshape(B, j0, MOBA_BLOCK, H, dh)
    vb = v_all[:, : j0 * MOBA_BLOCK].reshape(B, j0, MOBA_BLOCK, H, dh)
    kmean = jnp.mean(kb.astype(F32), axis=2)
    own_pos = j0 * MOBA_BLOCK + jnp.arange(L - j0 * MOBA_BLOCK)
    q_pos = start + jnp.arange(T)
    return moba_core(q, q_pos, kb, vb, kmean, k_all[:, j0 * MOBA_BLOCK:], v_all[:, j0 * MOBA_BLOCK:],
                     own_pos, j0, n_sel, bias_t)


def diff_core(q, q_pos, k, v, k_pos, lam, bias_t):
    scale = DIFF_QK ** -0.5
    bias = bias_t[:, t5_bucket(q_pos[:, None] - k_pos[None, :])][None]
    mask = k_pos[None, :] <= q_pos[:, None]

    def attn_map(qq, kk):
        logits = jnp.einsum('bqhd,bkhd->bhqk', qq, kk, preferred_element_type=F32) * scale + bias
        return jax.nn.softmax(jnp.where(mask, logits, NEG_INF), axis=-1)

    a = attn_map(q[..., :DIFF_QK], k[..., :DIFF_QK]) - lam * attn_map(q[..., DIFF_QK:], k[..., DIFF_QK:])
    return jnp.einsum('bhqk,bkhd->bqhd', a.astype(v.dtype), v)


def diff_prompt(q, k, v, lam, bias_t):
    B, S, H, dh = q.shape
    nq = S // Q_BLOCK
    k_pos = jnp.arange(S)
    q_blocks = jnp.swapaxes(q.reshape(B, nq, Q_BLOCK, H, dh), 0, 1)

    def step(args):
        qi, i = args
        return diff_core(qi, i * Q_BLOCK + jnp.arange(Q_BLOCK), k, v, k_pos, lam, bias_t)

    out = lax.map(step, (q_blocks, jnp.arange(nq, dtype=jnp.int32)))
    return jnp.swapaxes(out, 0, 1).reshape(B, S, H, dh)


def gmlp(u, vg, w_s, b_s):
    B, T, W = vg.shape
    nc = -(-T // GMLP_CHUNK)
    vc = jnp.pad(vg, ((0, 0), (0, nc * GMLP_CHUNK - T), (0, 0)))
    vc = vc.reshape(B, nc, GMLP_CHUNK, GMLP_GROUPS, HEAD_DIM)
    w = w_s * jnp.tril(jnp.ones((GMLP_CHUNK, GMLP_CHUNK), w_s.dtype))
    mixed = jnp.einsum('gts,bcsgd->bctgd', w, vc) + b_s.T[None, None, :, :, None]
    return u * mixed.reshape(B, nc * GMLP_CHUNK, W)[:, :T]


def swiglu(h, wg, wu, wd):
    return (jax.nn.silu(h @ wg) * (h @ wu)) @ wd


def moe(h, w_router, wg, wu, wd):
    logits = (h @ w_router).astype(F32)
    top_v, top_i = lax.top_k(logits, TOP_K)
    w = jax.nn.softmax(top_v, axis=-1)
    combine = jnp.sum(jax.nn.one_hot(top_i, N_EXPERTS, dtype=F32) * w[..., None], axis=-2).astype(h.dtype)
    out = jnp.zeros_like(h)
    for e in range(N_EXPERTS):
        out = out + combine[..., e:e + 1] * swiglu(h, wg[e], wu[e], wd[e])
    return out


def token_mixers(hn, w_in, w_out, bias_table, lq1, lk1, lq2, lk2, lam_init, diff_g, w_s, b_s, past_k, past_v):
    B, T, _ = hn.shape
    z = hn @ w_in
    parts = jnp.split(z, np.cumsum(PROJ_SIZES)[:-1].tolist(), axis=-1)
    qm, km, vm = [t.reshape(B, T, MOBA_HEADS, HEAD_DIM) for t in parts[0:3]]
    qd, kd, vd = [t.reshape(B, T, DIFF_HEADS, HEAD_DIM) for t in parts[3:6]]
    u = jax.nn.gelu(parts[6])
    vg = jax.nn.gelu(parts[7])
    bias_m = bias_table[:, :MOBA_HEADS].T
    bias_d = bias_table[:, MOBA_HEADS:].T
    lam = (jnp.exp(jnp.sum(lq1.astype(F32) * lk1.astype(F32)))
           - jnp.exp(jnp.sum(lq2.astype(F32) * lk2.astype(F32))) + lam_init)
    if past_k is None:
        om = moba_prompt(qm, km, vm, bias_m)
        od = diff_prompt(qd, kd, vd, lam, bias_d)
    else:
        start = past_k.shape[1]
        km_all = jnp.concatenate([past_k[:, :, :MOBA_HEADS], km], axis=1)
        vm_all = jnp.concatenate([past_v[:, :, :MOBA_HEADS], vm], axis=1)
        kd_all = jnp.concatenate([past_k[:, :, MOBA_HEADS:], kd], axis=1)
        vd_all = jnp.concatenate([past_v[:, :, MOBA_HEADS:], vd], axis=1)
        om = moba_sample(qm, km_all, vm_all, bias_m)
        od = diff_core(qd, start + jnp.arange(T), kd_all, vd_all, jnp.arange(start + T), lam, bias_d)
    od = rmsnorm(od, diff_g.reshape(DIFF_HEADS, HEAD_DIM)) * (1.0 - lam_init)
    og = gmlp(u, vg, w_s, b_s)
    mixed = jnp.concatenate([om.reshape(B, T, MOBA_W), od.reshape(B, T, DIFF_W), og], axis=-1)
    k_rows = jnp.concatenate([km, kd], axis=2)
    v_rows = jnp.concatenate([vm, vd], axis=2)
    return mixed @ w_out, k_rows, v_rows, vg


def setup_inputs(seed: int = 0) -> dict:
    key = jax.random.key(seed)
    ks = iter(jax.random.split(key, 40))
    n_pages = PAST_LEN // PAGE_SIZE
    n_pool = (DEC_BATCH * n_pages * 5) // 4
    n_dense = (DEPTH + 1) // 2
    n_moe = DEPTH // 2

    def nrm(shape, scale):
        return jax.random.normal(next(ks), shape, F32) * scale

    def gain(shape):
        return 1.0 + nrm(shape, 0.05)

    x_prompt = nrm((BATCH, SEQ, D_MODEL), 1.0)
    x_sample = nrm((DEC_BATCH, DEC_SEQ, D_MODEL), 1.0)
    cache_k = nrm((DEPTH, n_pool, PAGE_SIZE, KV_HEADS, HEAD_DIM), 1.0)
    cache_v = nrm((DEPTH, n_pool, PAGE_SIZE, KV_HEADS, HEAD_DIM), 1.0)
    page_table = jax.random.permutation(next(ks), n_pool)[: DEC_BATCH * n_pages]
    page_table = page_table.reshape(DEC_BATCH, n_pages).astype(jnp.int32)
    p_prompt = nrm((DEPTH, BATCH, SEQ, D_PLE), 1.0)
    p_sample = nrm((DEPTH, DEC_BATCH, DEC_SEQ, D_PLE), 1.0)
    return {
        'x_prompt': x_prompt,
        'x_sample': x_sample,
        'cache_k': cache_k,
        'cache_v': cache_v,
        'page_table': page_table,
        'p_prompt': p_prompt,
        'p_sample': p_sample,
        'bias_table': nrm((NUM_BUCKETS, KV_HEADS), 0.2),
        'norm_mix': gain((DEPTH, D_MODEL)),
        'w_in': nrm((DEPTH, D_MODEL, PROJ_W), D_MODEL ** -0.5),
        'w_out': nrm((DEPTH, MIX_W, D_MODEL), MIX_W ** -0.5),
        'lambda_q1': nrm((DEPTH, DIFF_QK), 0.1),
        'lambda_k1': nrm((DEPTH, DIFF_QK), 0.1),
        'lambda_q2': nrm((DEPTH, DIFF_QK), 0.1),
        'lambda_k2': nrm((DEPTH, DIFF_QK), 0.1),
        'diff_norm': gain((DEPTH, DIFF_W)),
        'gmlp_w_s': nrm((DEPTH, GMLP_GROUPS, GMLP_CHUNK, GMLP_CHUNK), GMLP_CHUNK ** -0.5),
        'gmlp_b_s': 1.0 + nrm((DEPTH, GMLP_GROUPS, GMLP_CHUNK), 0.1),
        'norm_ffn': gain((DEPTH, D_MODEL)),
        'w_ffn_gate': nrm((n_dense, D_MODEL, D_FF), D_MODEL ** -0.5),
        'w_ffn_up': nrm((n_dense, D_MODEL, D_FF), D_MODEL ** -0.5),
        'w_ffn_down': nrm((n_dense, D_FF, D_MODEL), D_FF ** -0.5),
        'w_router': nrm((n_moe, D_MODEL, N_EXPERTS), D_MODEL ** -0.5),
        'w_exp_gate': nrm((n_moe, N_EXPERTS, D_MODEL, D_FF), D_MODEL ** -0.5),
        'w_exp_up': nrm((n_moe, N_EXPERTS, D_MODEL, D_FF), D_MODEL ** -0.5),
        'w_exp_down': nrm((n_moe, N_EXPERTS, D_FF, D_MODEL), D_FF ** -0.5),
        'norm_ple': gain((DEPTH, D_MODEL)),
        'w_ple_gate': nrm((DEPTH, D_MODEL, D_MODEL), D_MODEL ** -0.5),
        'w_ple_proj': nrm((DEPTH, D_PLE, D_MODEL), D_PLE ** -0.5),
        'norm_final': gain((D_MODEL,)),
    }


def reference(x_prompt, x_sample, cache_k, cache_v, page_table, p_prompt, p_sample, bias_table,
              norm_mix, w_in, w_out, lambda_q1, lambda_k1, lambda_q2, lambda_k2, diff_norm,
              gmlp_w_s, gmlp_b_s, norm_ffn, w_ffn_gate, w_ffn_up, w_ffn_down, w_router,
              w_exp_gate, w_exp_up, w_exp_down, norm_ple, w_ple_gate, w_ple_proj, norm_final):
    def trunk(h, p, paged):
        new_k, new_v, new_g = [], [], []
        for i in range(DEPTH):
            past_k = past_v = None
            if paged:
                n_seq, n_pages = page_table.shape
                past_k = cache_k[i][page_table].reshape(n_seq, n_pages * PAGE_SIZE, KV_HEADS, HEAD_DIM)
                past_v = cache_v[i][page_table].reshape(n_seq, n_pages * PAGE_SIZE, KV_HEADS, HEAD_DIM)
            mixed, k_rows, v_rows, g_rows = token_mixers(
                rmsnorm(h, norm_mix[i]), w_in[i], w_out[i], bias_table,
                lambda_q1[i], lambda_k1[i], lambda_q2[i], lambda_k2[i], lambda_init(i),
                diff_norm[i], gmlp_w_s[i], gmlp_b_s[i], past_k, past_v)
            h = h + mixed
            hn = rmsnorm(h, norm_ffn[i])
            if i % 2 == 0:
                h = h + swiglu(hn, w_ffn_gate[i // 2], w_ffn_up[i // 2], w_ffn_down[i // 2])
            else:
                h = h + moe(hn, w_router[i // 2], w_exp_gate[i // 2], w_exp_up[i // 2], w_exp_down[i // 2])
            gate = jax.nn.sigmoid(rmsnorm(h, norm_ple[i]) @ w_ple_gate[i])
            h = h + gate * (p[i] @ w_ple_proj[i])
            new_k.append(k_rows)
            new_v.append(v_rows)
            new_g.append(g_rows)
        return rmsnorm(h, norm_final), jnp.stack(new_k), jnp.stack(new_v), new_g

    y_prompt, new_k_prompt, new_v_prompt, _ = trunk(x_prompt, p_prompt, False)
    y_sample, new_k_sample, new_v_sample, g_sample = trunk(x_sample, p_sample, True)
    gmlp_v_sample = jnp.stack(g_sample)
    return (y_prompt, y_sample, new_k_prompt, new_v_prompt, new_k_sample, new_v_sample, gmlp_v_sample)
```

```python
import functools
import math

import jax
import jax.numpy as jnp
import numpy as np
from jax import lax
from jax.experimental import pallas as pl
from jax.experimental.pallas import tpu as pltpu

F32 = jnp.float32
BF16 = jnp.bfloat16

D_MODEL = 1024
HEAD_DIM = 64
MOBA_HEADS = 6
DIFF_HEADS = 4
KV_HEADS = MOBA_HEADS + DIFF_HEADS
MOBA_W = MOBA_HEADS * HEAD_DIM
DIFF_W = DIFF_HEADS * HEAD_DIM
GMLP_GROUPS = 6
GMLP_W = GMLP_GROUPS * HEAD_DIM
QKV_W = MOBA_W + DIFF_W
DIFF_QK = HEAD_DIM // 2
MOBA_BLOCK = 256
MOBA_TOPK = 3
GMLP_CHUNK = 128
NUM_BUCKETS = 32
MAX_EXACT = 16
MAX_DISTANCE = 2048
D_FF = 2816
N_EXPERTS = 8
TOP_K = 2
D_PLE = 256
PAGE_SIZE = 128
RMS_EPS = 1e-6

LANES = 128
NEG = -1e30
FF_CHUNK = D_FF // 2
N_BIAS_TILES = 8
VMEM_LIMIT = 56 * 1024 * 1024


def _cparams(sem):
    return pltpu.CompilerParams(dimension_semantics=sem, vmem_limit_bytes=VMEM_LIMIT)


def _rms(x, g):
    return x * lax.rsqrt(jnp.mean(x * x, axis=-1, keepdims=True) + RMS_EPS) * g


def _bucket_np(dist):
    n = np.maximum(dist, 0)
    nf = np.maximum(n, MAX_EXACT).astype(np.float32)
    large = MAX_EXACT + (np.log(nf / np.float32(MAX_EXACT)) / np.float32(math.log(MAX_DISTANCE / MAX_EXACT))
                         * np.float32(NUM_BUCKETS - MAX_EXACT)).astype(np.int32)
    return np.where(n < MAX_EXACT, n, np.minimum(large, NUM_BUCKETS - 1)).astype(np.int32)


def _prompt_bias_tiles(bias_table):
    qi = np.arange(MOBA_BLOCK)[:, None]
    kj = np.arange(MOBA_BLOCK)[None, :]
    dist = np.stack([qi - kj + MOBA_BLOCK * d for d in range(N_BIAS_TILES)])
    assert _bucket_np(np.array([MOBA_BLOCK * (N_BIAS_TILES - 1) - MOBA_BLOCK + 1]))[0] == NUM_BUCKETS - 1
    tiles = jnp.take(bias_table.astype(F32), jnp.asarray(_bucket_np(dist)), axis=0)
    tiles = jnp.where(jnp.asarray(dist >= 0)[..., None], tiles, NEG)
    return jnp.transpose(tiles, (3, 0, 1, 2))


def _inproj_kernel(x_ref, g_ref, w_ref, qs_ref, ws_ref, bs_ref,
                   q_ref, kf_ref, vf_ref, kb_ref, vb_ref, og_ref, vg_ref, km_ref, *, tm):
    xn = _rms(x_ref[...], g_ref[...]).astype(BF16)
    c0, c1, c2, c3, c4 = 0, QKV_W, 2 * QKV_W, 3 * QKV_W, 3 * QKV_W + GMLP_W
    q = jnp.dot(xn, w_ref[:, c0:c1], preferred_element_type=F32)
    q_ref[...] = (q * qs_ref[...]).astype(BF16)
    k = jnp.dot(xn, w_ref[:, c1:c2], preferred_element_type=F32)
    kf_ref[...] = k
    kb_ref[...] = k.astype(BF16)
    for j in range(tm // MOBA_BLOCK):
        km_ref[j] = jnp.mean(k[j * MOBA_BLOCK:(j + 1) * MOBA_BLOCK], axis=0, keepdims=True)
    v = jnp.dot(xn, w_ref[:, c2:c3], preferred_element_type=F32)
    vf_ref[...] = v
    vb_ref[...] = v.astype(BF16)
    u = jax.nn.gelu(jnp.dot(xn, w_ref[:, c3:c4], preferred_element_type=F32))
    vg = jax.nn.gelu(jnp.dot(xn, w_ref[:, c4:], preferred_element_type=F32))
    vg_ref[...] = vg
    vgb = vg.astype(BF16)
    first = lax.broadcasted_iota(jnp.int32, (1, LANES), 1) < HEAD_DIM
    for c in range(tm // GMLP_CHUNK):
        rows = slice(c * GMLP_CHUNK, (c + 1) * GMLP_CHUNK)
        for p in range(GMLP_GROUPS // 2):
            cols = slice(p * LANES, (p + 1) * LANES)
            vc = vgb[rows, cols]
            m0 = jnp.dot(ws_ref[2 * p], vc, preferred_element_type=F32)
            m1 = jnp.dot(ws_ref[2 * p + 1], vc, preferred_element_type=F32)
            mixed = jnp.where(first, m0, m1) + bs_ref[:, cols]
            og_ref[rows, cols] = (u[rows, cols] * mixed).astype(BF16)


def _inproj(x, g, w_perm, q_scale, w_mix, b_mix, *, tm):
    T = x.shape[0]
    assert T % tm == 0 and tm % MOBA_BLOCK == 0
    nkm = tm // MOBA_BLOCK
    row = lambda w: pl.BlockSpec((tm, w), lambda i: (i, 0))
    full = lambda a: pl.BlockSpec(a.shape, lambda i: (0,) * a.ndim)
    out_shape = (
        jax.ShapeDtypeStruct((T, QKV_W), BF16),
        jax.ShapeDtypeStruct((T, QKV_W), F32),
        jax.ShapeDtypeStruct((T, QKV_W), F32),
        jax.ShapeDtypeStruct((T, QKV_W), BF16),
        jax.ShapeDtypeStruct((T, QKV_W), BF16),
        jax.ShapeDtypeStruct((T, GMLP_W), BF16),
        jax.ShapeDtypeStruct((T, GMLP_W), F32),
        jax.ShapeDtypeStruct((T // MOBA_BLOCK, 1, QKV_W), F32),
    )
    out_specs = (row(QKV_W), row(QKV_W), row(QKV_W), row(QKV_W), row(QKV_W), row(GMLP_W), row(GMLP_W),
                 pl.BlockSpec((nkm, 1, QKV_W), lambda i: (i, 0, 0)))
    return pl.pallas_call(
        functools.partial(_inproj_kernel, tm=tm),
        grid=(T // tm,),
        in_specs=[row(D_MODEL), full(g), full(w_perm), full(q_scale), full(w_mix), full(b_mix)],
        out_specs=out_specs,
        out_shape=out_shape,
        compiler_params=_cparams(("parallel",)),
        name="inproj",
    )(x, g, w_perm, q_scale, w_mix, b_mix)


def _softmax_first(s, v, m_ref, l_ref, acc_ref, j):
    m = jnp.max(s, axis=1, keepdims=True)
    p = jnp.exp(s - m)
    m_ref[j] = m
    l_ref[j] = jnp.sum(p, axis=1, keepdims=True)
    acc_ref[j] = jnp.dot(p.astype(BF16), v, preferred_element_type=F32)


def _softmax_update(s, v, m_ref, l_ref, acc_ref, j):
    m_old = m_ref[j]
    m = jnp.maximum(m_old, jnp.max(s, axis=1, keepdims=True))
    alpha = jnp.exp(m_old - m)
    p = jnp.exp(s - m)
    m_ref[j] = m
    l_ref[j] = alpha * l_ref[j] + jnp.sum(p, axis=1, keepdims=True)
    acc_ref[j] = alpha * acc_ref[j] + jnp.dot(p.astype(BF16), v, preferred_element_type=F32)


def _qk(qm, k):
    return lax.dot_general(qm, k, (((1,), (1,)), ((), ())), preferred_element_type=F32)


def _split3(x):
    hi = x.astype(BF16)
    r = x - hi.astype(F32)
    mid = r.astype(BF16)
    lo = (r - mid.astype(F32)).astype(BF16)
    return hi, mid, lo


def _top3_mask(gate, col):
    sel = jnp.zeros(gate.shape, F32)
    for _ in range(MOBA_TOPK):
        m = jnp.max(gate, axis=1, keepdims=True)
        idx = jnp.min(jnp.where(gate == m, col, LANES), axis=1, keepdims=True)
        hit = col == idx
        sel = jnp.where(hit, 1.0, sel)
        gate = jnp.where(hit, -jnp.inf, gate)
    return sel


def _moba_prompt_kernel(q_ref, k_ref, v_ref, km_ref, bias_ref, o_ref, m_ref, l_ref, acc_ref, sel_ref):
    i = pl.program_id(2)
    blk = MOBA_BLOCK
    q = q_ref[0]
    lane = lax.broadcasted_iota(jnp.int32, (1, LANES), 1)
    col = lax.broadcasted_iota(jnp.int32, (blk, LANES), 1)
    km_hi, km_mid, km_lo = _split3(km_ref[0])
    own = pl.ds(pl.multiple_of(i * blk, blk), blk)
    k_own, v_own = k_ref[0, own, :], v_ref[0, own, :]
    qms = []
    for h in range(2):
        qm = jnp.where(lane // HEAD_DIM == h, q, jnp.zeros_like(q))
        qms.append(qm)
        gate = _qk(qm, km_hi) + _qk(qm, km_mid) + _qk(qm, km_lo)
        gate = jnp.where(col < i, gate, -jnp.inf)
        sel_ref[h] = jnp.where(col < i, _top3_mask(gate, col), 0.0)
        _softmax_first(_qk(qm, k_own) + bias_ref[0, h, 0], v_own, m_ref, l_ref, acc_ref, h)

    def body(n, carry):
        rows = pl.ds(pl.multiple_of(n * blk, blk), blk)
        kb, vb = k_ref[0, rows, :], v_ref[0, rows, :]
        d = jnp.minimum(i - n, N_BIAS_TILES - 1)
        for h in range(2):
            picked = jnp.sum(jnp.where(col == n, sel_ref[h], 0.0), axis=1, keepdims=True)
            s = _qk(qms[h], kb) + bias_ref[0, h, d]
            s = jnp.where(picked > 0.5, s, NEG)
            _softmax_update(s, vb, m_ref, l_ref, acc_ref, h)
        return carry

    lax.fori_loop(0, i, body, 0)
    o0 = acc_ref[0] / l_ref[0]
    o1 = acc_ref[1] / l_ref[1]
    o_ref[0] = jnp.where(lane < HEAD_DIM, o0, o1).astype(BF16)


def _moba_prompt(q, k, v, kmean, bias_tiles):
    B, S, _ = q.shape
    blk = MOBA_BLOCK
    nq = S // blk
    npair = MOBA_HEADS // 2
    return pl.pallas_call(
        _moba_prompt_kernel,
        grid=(B, npair, nq),
        in_specs=[
            pl.BlockSpec((1, blk, LANES), lambda b, p, i: (b, i, p)),
            pl.BlockSpec((1, S, LANES), lambda b, p, i: (b, 0, p)),
            pl.BlockSpec((1, S, LANES), lambda b, p, i: (b, 0, p)),
            pl.BlockSpec((1, LANES, LANES), lambda b, p, i: (b, 0, p)),
            pl.BlockSpec((1, 2, N_BIAS_TILES, blk, blk), lambda b, p, i: (p, 0, 0, 0, 0)),
        ],
        out_specs=pl.BlockSpec((1, blk, LANES), lambda b, p, i: (b, i, p)),
        out_shape=jax.ShapeDtypeStruct((B, S, MOBA_W), BF16),
        scratch_shapes=[pltpu.VMEM((2, blk, 1), F32), pltpu.VMEM((2, blk, 1), F32),
                        pltpu.VMEM((2, blk, LANES), F32), pltpu.VMEM((2, blk, LANES), F32)],
        compiler_params=_cparams(("parallel", "parallel", "arbitrary")),
        name="moba_prompt",
    )(q, k, v, kmean, bias_tiles)


def _diff_finish(acc_ref, l_ref, lam, g, out_scale, lane):
    outs = []
    for h in range(2):
        o = acc_ref[2 * h] / l_ref[2 * h] - lam * (acc_ref[2 * h + 1] / l_ref[2 * h + 1])
        mine = lane // HEAD_DIM == h
        ms = jnp.sum(jnp.where(mine, o * o, 0.0), axis=1, keepdims=True) * (1.0 / HEAD_DIM)
        outs.append(o * lax.rsqrt(ms + RMS_EPS))
    return jnp.where(lane < HEAD_DIM, outs[0], outs[1]) * g * out_scale


def _diff_prompt_kernel(lam_ref, q_ref, k_ref, v_ref, bias_ref, g_ref, o_ref, m_ref, l_ref, acc_ref, *, out_scale):
    i = pl.program_id(2)
    blk = MOBA_BLOCK
    q = q_ref[0]
    lane = lax.broadcasted_iota(jnp.int32, (1, LANES), 1)
    own = pl.ds(pl.multiple_of(i * blk, blk), blk)
    k_own, v_own = k_ref[0, own, :], v_ref[0, own, :]
    qms = [jnp.where(lane // DIFF_QK == j, q, jnp.zeros_like(q)) for j in range(4)]
    for j in range(4):
        _softmax_first(_qk(qms[j], k_own) + bias_ref[0, j // 2, 0], v_own, m_ref, l_ref, acc_ref, j)

    def body(n, carry):
        rows = pl.ds(pl.multiple_of(n * blk, blk), blk)
        kb, vb = k_ref[0, rows, :], v_ref[0, rows, :]
        d = jnp.minimum(i - n, N_BIAS_TILES - 1)
        for j in range(4):
            _softmax_update(_qk(qms[j], kb) + bias_ref[0, j // 2, d], vb, m_ref, l_ref, acc_ref, j)
        return carry

    lax.fori_loop(0, i, body, 0)
    o_ref[0] = _diff_finish(acc_ref, l_ref, lam_ref[0, 0], g_ref[0], out_scale, lane).astype(BF16)


def _diff_prompt(q, k, v, bias_tiles, lam, g, out_scale):
    B, S, _ = q.shape
    blk = MOBA_BLOCK
    nq = S // blk
    npair = DIFF_HEADS // 2
    off = MOBA_HEADS // 2
    return pl.pallas_call(
        functools.partial(_diff_prompt_kernel, out_scale=out_scale),
        grid=(B, npair, nq),
        in_specs=[
            pl.BlockSpec(memory_space=pltpu.SMEM),
            pl.BlockSpec((1, blk, LANES), lambda b, p, i: (b, i, p + off)),
            pl.BlockSpec((1, S, LANES), lambda b, p, i: (b, 0, p + off)),
            pl.BlockSpec((1, S, LANES), lambda b, p, i: (b, 0, p + off)),
            pl.BlockSpec((1, 2, N_BIAS_TILES, blk, blk), lambda b, p, i: (p + off, 0, 0, 0, 0)),
            pl.BlockSpec((1, 1, LANES), lambda b, p, i: (p, 0, 0)),
        ],
        out_specs=pl.BlockSpec((1, blk, LANES), lambda b, p, i: (b, i, p)),
        out_shape=jax.ShapeDtypeStruct((B, S, DIFF_W), BF16),
        scratch_shapes=[pltpu.VMEM((4, blk, 1), F32), pltpu.VMEM((4, blk, 1), F32),
                        pltpu.VMEM((4, blk, LANES), F32)],
        compiler_params=_cparams(("parallel", "parallel", "arbitrary")),
        name="diff_prompt",
    )(lam, q, k, v, bias_tiles, g)


def _post_a_kernel(x_ref, om_ref, od_ref, og_ref, wo_ref, g_ref, *rest):
    h = x_ref[...]
    h = h + jnp.dot(om_ref[...], wo_ref[0:MOBA_W, :], preferred_element_type=F32)
    h = h + jnp.dot(od_ref[...], wo_ref[MOBA_W:QKV_W, :], preferred_element_type=F32)
    h = h + jnp.dot(og_ref[...], wo_ref[QKV_W:, :], preferred_element_type=F32)
    if len(rest) == 2:
        h_ref, hn_ref = rest
    else:
        wr_ref, h_ref, hn_ref, route_ref = rest
    h_ref[...] = h
    hn = _rms(h, g_ref[...])
    hn_ref[...] = hn.astype(BF16)
    if len(rest) == 2:
        return
    a_hi, a_mid, _ = _split3(hn)
    dot = lambda a, b: jnp.dot(a, b, preferred_element_type=F32)
    logits = dot(a_hi, wr_ref[0]) + (dot(a_hi, wr_ref[1]) + dot(a_mid, wr_ref[0]))
    col = lax.broadcasted_iota(jnp.int32, logits.shape, 1)
    logits = jnp.where(col < N_EXPERTS, logits, -jnp.inf)
    m1 = jnp.max(logits, axis=1, keepdims=True)
    i1 = jnp.min(jnp.where(logits == m1, col, LANES), axis=1, keepdims=True)
    rest = jnp.where(col == i1, -jnp.inf, logits)
    m2 = jnp.max(rest, axis=1, keepdims=True)
    i2 = jnp.min(jnp.where(rest == m2, col, LANES), axis=1, keepdims=True)
    e = jnp.exp(m2 - m1)
    w1 = 1.0 / (1.0 + e)
    w2 = e / (1.0 + e)
    route = jnp.where(col == 0, i1.astype(F32), 0.0)
    route = jnp.where(col == 1, i2.astype(F32), route)
    route = jnp.where(col == 2, w1, route)
    route = jnp.where(col == 3, w2, route)
    route_ref[...] = route


def _post_a(x, om, od, og, w_out, g, w_router3, *, tm):
    T = x.shape[0]
    row = lambda w: pl.BlockSpec((tm, w), lambda i: (i, 0))
    full = lambda a: pl.BlockSpec(a.shape, lambda i: (0,) * a.ndim)
    routed = w_router3 is not None
    outs = pl.pallas_call(
        _post_a_kernel,
        grid=(T // tm,),
        in_specs=[row(D_MODEL), row(MOBA_W), row(DIFF_W), row(GMLP_W), full(w_out), full(g)]
                 + ([full(w_router3)] if routed else []),
        out_specs=(row(D_MODEL), row(D_MODEL)) + ((row(LANES),) if routed else ()),
        out_shape=(jax.ShapeDtypeStruct((T, D_MODEL), F32), jax.ShapeDtypeStruct((T, D_MODEL), BF16))
                  + ((jax.ShapeDtypeStruct((T, LANES), F32),) if routed else ()),
        compiler_params=_cparams(("parallel",)),
        name="post_a",
    )(x, om, od, og, w_out, g, *([w_router3] if routed else []))
    return outs if routed else (*outs, None)


def _ffn_kernel(te_ref, nu_ref, x_ref, wg_ref, wu_ref, wd_ref, sc_ref, o_ref, acc_ref):
    t = pl.program_id(0)
    f = pl.program_id(1)

    @pl.when(t < nu_ref[0])
    def _():
        x = x_ref[...]
        g = jnp.dot(x, wg_ref[0], preferred_element_type=F32)
        u = jnp.dot(x, wu_ref[0], preferred_element_type=F32)
        a = (g * jax.nn.sigmoid(g) * u).astype(BF16)
        part = jnp.dot(a, wd_ref[0], preferred_element_type=F32)

        @pl.when(f == 0)
        def _():
            acc_ref[...] = part

        @pl.when(f == 1)
        def _():
            o_ref[...] = (acc_ref[...] + part) * sc_ref[...]

    @pl.when(jnp.logical_and(t >= nu_ref[0], f == 1))
    def _():
        o_ref[...] = jnp.zeros_like(o_ref)


def _ffn(tile_expert, n_used, x, wg, wu, wd, scale, *, tm):
    N = x.shape[0]
    n_tiles = N // tm
    chunk = lambda t, f: jnp.where(t % 2 == 0, f, 1 - f)
    return pl.pallas_call(
        _ffn_kernel,
        grid_spec=pltpu.PrefetchScalarGridSpec(
            num_scalar_prefetch=2,
            grid=(n_tiles, 2),
            in_specs=[
                pl.BlockSpec((tm, D_MODEL), lambda t, f, te, nu: (t, 0)),
                pl.BlockSpec((1, D_MODEL, FF_CHUNK), lambda t, f, te, nu: (te[t], 0, chunk(t, f))),
                pl.BlockSpec((1, D_MODEL, FF_CHUNK), lambda t, f, te, nu: (te[t], 0, chunk(t, f))),
                pl.BlockSpec((1, FF_CHUNK, D_MODEL), lambda t, f, te, nu: (te[t], chunk(t, f), 0)),
                pl.BlockSpec((tm, 1), lambda t, f, te, nu: (t, 0)),
            ],
            out_specs=pl.BlockSpec((tm, D_MODEL), lambda t, f, te, nu: (t, 0)),
            scratch_shapes=[pltpu.VMEM((tm, D_MODEL), F32)],
        ),
        out_shape=jax.ShapeDtypeStruct((N, D_MODEL), F32),
        compiler_params=_cparams(("arbitrary", "arbitrary")),
        name="ffn",
    )(tile_expert, n_used, x, wg, wu, wd, scale)


def _moe_dispatch(route, *, tm):
    T = route.shape[0]
    n_assign = T * TOP_K
    n_tiles = -(-n_assign // tm) + N_EXPERTS
    experts = route[:, :TOP_K].astype(jnp.int32).reshape(n_assign)
    weights = route[:, TOP_K:2 * TOP_K].reshape(n_assign)
    onehot = (experts[:, None] == jnp.arange(N_EXPERTS)[None, :]).astype(jnp.int32)
    rank = jnp.sum((jnp.cumsum(onehot, axis=0) - onehot) * onehot, axis=1)
    counts = jnp.sum(onehot, axis=0)
    tiles_per = (counts + tm - 1) // tm
    tile_end = jnp.cumsum(tiles_per)
    slot = ((tile_end - tiles_per) * tm)[experts] + rank
    tile_expert = jnp.minimum(jnp.searchsorted(tile_end, jnp.arange(n_tiles), side="right"),
                              N_EXPERTS - 1).astype(jnp.int32)
    n_used = tile_end[-1:].astype(jnp.int32)
    token = jnp.arange(n_assign, dtype=jnp.int32) // TOP_K
    slot_token = jnp.zeros((n_tiles * tm,), jnp.int32).at[slot].set(token)
    slot_scale = jnp.zeros((n_tiles * tm,), F32).at[slot].set(weights)
    return tile_expert, n_used, slot_token, slot_scale[:, None], slot.reshape(T, TOP_K)


def _post_b_kernel(*refs, n_y, final):
    h_ref = refs[0]
    y_refs = refs[1:1 + n_y]
    p_ref, gp_ref, wgate_ref, wproj_ref, gf_ref, o_ref = refs[1 + n_y:]
    h = h_ref[...]
    for y_ref in y_refs:
        h = h + y_ref[...]
    gate = jax.nn.sigmoid(jnp.dot(_rms(h, gp_ref[...]).astype(BF16), wgate_ref[...], preferred_element_type=F32))
    h = h + gate * jnp.dot(p_ref[...].astype(BF16), wproj_ref[...], preferred_element_type=F32)
    o_ref[...] = _rms(h, gf_ref[...]) if final else h


def _post_b(h, ys, p, g_ple, w_gate, w_proj, g_final, *, tm, final):
    T = h.shape[0]
    row = lambda w: pl.BlockSpec((tm, w), lambda i: (i, 0))
    full = lambda a: pl.BlockSpec(a.shape, lambda i: (0,) * a.ndim)
    return pl.pallas_call(
        functools.partial(_post_b_kernel, n_y=len(ys), final=final),
        grid=(T // tm,),
        in_specs=[row(D_MODEL)] + [row(D_MODEL)] * len(ys)
                 + [row(D_PLE), full(g_ple), full(w_gate), full(w_proj), full(g_final)],
        out_specs=row(D_MODEL),
        out_shape=jax.ShapeDtypeStruct((T, D_MODEL), F32),
        compiler_params=_cparams(("parallel",)),
        name="post_b",
    )(h, *ys, p, g_ple, w_gate, w_proj, g_final)


PAGES_PER_STEP = 8
MOBA_ROWS_PER_TOKEN = MOBA_HEADS
DIFF_MAPS = 2 * DIFF_HEADS


def _sample_attn_kernel(pt_ref, lam_ref, *refs, n_tok, n_steps, out_scale):
    P = PAGES_PER_STEP
    k_refs = refs[:P]
    v_refs = refs[P:2 * P]
    (q_ref, kn_ref, vn_ref, bm_ref, bd_ref, bmo_ref, bdo_ref, g_ref, o_ref,
     qm_ref, qg_ref, qd_ref, gate_ref, mb_ref, lb_ref, accb_ref, m_ref, l_ref, acc_ref) = refs[2 * P:]
    s = pl.program_id(1)
    rm = MOBA_HEADS * n_tok
    rd = DIFF_MAPS * n_tok
    blk = MOBA_BLOCK
    lane = lax.broadcasted_iota(jnp.int32, (1, LANES), 1)

    @pl.when(s == 0)
    def _():
        q = q_ref[0]
        qm = jnp.concatenate([q[:, :MOBA_W]] * MOBA_HEADS, axis=0)
        r = lax.broadcasted_iota(jnp.int32, qm.shape, 0) // n_tok
        c = lax.broadcasted_iota(jnp.int32, qm.shape, 1) // HEAD_DIM
        qm = jnp.where(r == c, qm, 0.0)
        qg_ref[...] = qm
        qm_ref[...] = qm.astype(BF16)
        qd = jnp.concatenate([q[:, MOBA_W:]] * DIFF_MAPS, axis=0)
        r = lax.broadcasted_iota(jnp.int32, qd.shape, 0) // n_tok
        c = lax.broadcasted_iota(jnp.int32, qd.shape, 1) // DIFF_QK
        qd_ref[...] = jnp.where(r == c, qd, 0.0).astype(BF16)
        gate_ref[...] = jnp.full(gate_ref.shape, -jnp.inf, F32)
        mb_ref[...] = jnp.full(mb_ref.shape, NEG, F32)
        lb_ref[...] = jnp.zeros(lb_ref.shape, F32)
        m_ref[...] = jnp.full(m_ref.shape, NEG, F32)
        l_ref[...] = jnp.zeros(l_ref.shape, F32)
        acc_ref[...] = jnp.zeros(acc_ref.shape, F32)

    def diff_update(kd, vd, bias):
        sd = _qk(qd_ref[...], kd) + bias
        m_old = m_ref[...]
        m = jnp.maximum(m_old, jnp.max(sd, axis=1, keepdims=True))
        alpha = jnp.exp(m_old - m)
        p = jnp.exp(sd - m)
        m_ref[...] = m
        l_ref[...] = alpha * l_ref[...] + jnp.sum(p, axis=1, keepdims=True)
        acc_ref[...] = alpha * acc_ref[...] + jnp.dot(p.astype(BF16), vd, preferred_element_type=F32)

    def moba_partial(km, vm, bias):
        sm = _qk(qm_ref[...], km) + bias
        m = jnp.max(sm, axis=1, keepdims=True)
        p = jnp.exp(sm - m)
        return m, jnp.sum(p, axis=1, keepdims=True), jnp.dot(p.astype(BF16), vm, preferred_element_type=F32)

    for bi in range(P // 2):
        n = s * (P // 2) + bi
        kf = jnp.concatenate([k_refs[2 * bi][...], k_refs[2 * bi + 1][...]], axis=0)
        vf = jnp.concatenate([v_refs[2 * bi][...], v_refs[2 * bi + 1][...]], axis=0)
        kb, vb = kf.astype(BF16), vf.astype(BF16)
        kmean = jnp.mean(kf[:, :MOBA_W], axis=0, keepdims=True)
        gcol = jnp.sum(qg_ref[...] * kmean, axis=1, keepdims=True)
        m, l, acc = moba_partial(kb[:, :MOBA_W], vb[:, :MOBA_W], bm_ref[n])
        gate_ref[...] = jnp.where(lane == n, gcol, gate_ref[...])
        mb_ref[...] = jnp.where(lane == n, m, mb_ref[...])
        lb_ref[...] = jnp.where(lane == n, l, lb_ref[...])
        accb_ref[n] = acc
        diff_update(kb[:, MOBA_W:], vb[:, MOBA_W:], bd_ref[n])

    @pl.when(s == n_steps - 1)
    def _():
        kn = kn_ref[0].astype(BF16)
        vn = vn_ref[0].astype(BF16)
        diff_update(kn[:, MOBA_W:], vn[:, MOBA_W:], bdo_ref[...])
        m_own, l_own, acc_own = moba_partial(kn[:, :MOBA_W], vn[:, :MOBA_W], bmo_ref[...])
        col = lax.broadcasted_iota(jnp.int32, (rm, LANES), 1)
        sel = _top3_mask(gate_ref[...], col)
        sel = jnp.where(gate_ref[...] > -jnp.inf, sel, 0.0)
        mb = mb_ref[...]
        m_fin = jnp.maximum(m_own, jnp.max(jnp.where(sel > 0.5, mb, NEG), axis=1, keepdims=True))
        w = sel * jnp.exp(jnp.minimum(mb - m_fin, 0.0))
        w_own = jnp.exp(m_own - m_fin)
        l_fin = w_own * l_own + jnp.sum(w * lb_ref[...], axis=1, keepdims=True)

        def merge(nb, acc):
            wn = jnp.sum(jnp.where(col == nb, w, 0.0), axis=1, keepdims=True)
            return acc + wn * accb_ref[nb]

        acc_fin = lax.fori_loop(0, n_steps * (P // 2), merge, w_own * acc_own)
        om_rows = acc_fin / l_fin
        lane_m = lax.broadcasted_iota(jnp.int32, (n_tok, MOBA_W), 1) // HEAD_DIM
        om = jnp.zeros((n_tok, MOBA_W), F32)
        for h in range(MOBA_HEADS):
            om = jnp.where(lane_m == h, om_rows[h * n_tok:(h + 1) * n_tok], om)

        od_rows = acc_ref[...] / l_ref[...]
        lane_d = lax.broadcasted_iota(jnp.int32, (n_tok, DIFF_W), 1) // HEAD_DIM
        lam = lam_ref[0, 0]
        od = jnp.zeros((n_tok, DIFF_W), F32)
        for h in range(DIFF_HEADS):
            a1 = od_rows[(2 * h) * n_tok:(2 * h + 1) * n_tok]
            a2 = od_rows[(2 * h + 1) * n_tok:(2 * h + 2) * n_tok]
            od = jnp.where(lane_d == h, a1 - lam * a2, od)
        normed = jnp.zeros_like(od)
        for h in range(DIFF_HEADS):
            ms = jnp.sum(jnp.where(lane_d == h, od * od, 0.0), axis=1, keepdims=True) * (1.0 / HEAD_DIM)
            normed = jnp.where(lane_d == h, od * lax.rsqrt(ms + RMS_EPS), normed)
        o_ref[0, :, :MOBA_W] = om
        o_ref[0, :, MOBA_W:] = normed * g_ref[...] * out_scale


def _sample_attn(layer, n_pool, page_table, cache_k2, cache_v2, q, k_new, v_new,
                 bias_m, bias_d, bias_mo, bias_do, lam, g, out_scale):
    nseq, n_pages = page_table.shape
    n_tok = q.shape[1]
    P = PAGES_PER_STEP
    assert n_pages % P == 0 and n_pages // 2 <= LANES and n_tok % 8 == 0
    n_steps = n_pages // P
    nb = n_pages // 2
    rm, rd = MOBA_HEADS * n_tok, DIFF_MAPS * n_tok
    page = lambda j: pl.BlockSpec((PAGE_SIZE, QKV_W), lambda b, s, pt, j=j: (layer * n_pool + pt[b, s * P + j], 0))
    const = lambda a: pl.BlockSpec(a.shape, lambda b, s, pt: (0,) * a.ndim)
    per_seq = lambda a: pl.BlockSpec((1,) + a.shape[1:], lambda b, s, pt: (b,) + (0,) * (a.ndim - 1))
    return pl.pallas_call(
        functools.partial(_sample_attn_kernel, n_tok=n_tok, n_steps=n_steps, out_scale=out_scale),
        grid_spec=pltpu.PrefetchScalarGridSpec(
            num_scalar_prefetch=1,
            grid=(nseq, n_steps),
            in_specs=[pl.BlockSpec(memory_space=pltpu.SMEM)]
                     + [page(j) for j in range(P)] + [page(j) for j in range(P)]
                     + [per_seq(q), per_seq(k_new), per_seq(v_new),
                        const(bias_m), const(bias_d), const(bias_mo), const(bias_do), const(g)],
            out_specs=pl.BlockSpec((1, n_tok, QKV_W), lambda b, s, pt: (b, 0, 0)),
            scratch_shapes=[
                pltpu.VMEM((rm, MOBA_W), BF16), pltpu.VMEM((rm, MOBA_W), F32), pltpu.VMEM((rd, DIFF_W), BF16),
                pltpu.VMEM((rm, LANES), F32), pltpu.VMEM((rm, LANES), F32), pltpu.VMEM((rm, LANES), F32),
                pltpu.VMEM((nb, rm, MOBA_W), F32),
                pltpu.VMEM((rd, 1), F32), pltpu.VMEM((rd, 1), F32), pltpu.VMEM((rd, DIFF_W), F32),
            ],
        ),
        out_shape=jax.ShapeDtypeStruct((nseq, n_tok, QKV_W), F32),
        compiler_params=_cparams(("parallel", "arbitrary")),
        name="sample_attn",
    )(page_table, lam, *([cache_k2] * P), *([cache_v2] * P), q, k_new, v_new,
      bias_m, bias_d, bias_mo, bias_do, g)


def _sample_bias(bias_table, past_len, n_tok):
    t = np.arange(n_tok)
    d_past = _bucket_np(past_len + t[:, None] - np.arange(past_len)[None, :])
    d_own = t[:, None] - np.arange(LANES)[None, :]
    valid = (d_own >= 0) & (np.arange(LANES)[None, :] < n_tok)
    tab = bias_table.astype(F32).T
    past = tab[:, jnp.asarray(d_past)]
    own = jnp.where(jnp.asarray(valid)[None], tab[:, jnp.asarray(_bucket_np(d_own))], NEG)
    stack_m = lambda a: a[:MOBA_HEADS].reshape(MOBA_HEADS * n_tok, -1)
    stack_d = lambda a: jnp.repeat(a[MOBA_HEADS:], 2, axis=0).reshape(DIFF_MAPS * n_tok, -1)
    blocks = lambda a: jnp.transpose(a.reshape(a.shape[0], past_len // MOBA_BLOCK, MOBA_BLOCK), (1, 0, 2))
    return blocks(stack_m(past)), blocks(stack_d(past)), stack_m(own), stack_d(own)


def _lambda_init(i):
    return 0.8 - 0.6 * math.exp(-0.3 * i)


def _perm_w_in(w):
    sizes = (MOBA_W, MOBA_W, MOBA_W, DIFF_W, DIFF_W, DIFF_W, GMLP_W, GMLP_W)
    o = np.concatenate([[0], np.cumsum(sizes)])
    qm, km, vm, qd, kd, vd, u, vg = [w[:, o[j]:o[j + 1]] for j in range(8)]
    return jnp.concatenate([qm, qd, km, kd, vm, vd, u, vg], axis=1).astype(BF16)


def _pad_router(w):
    wp = jnp.pad(w.astype(F32), ((0, 0), (0, LANES - N_EXPERTS)))
    return jnp.stack(_split3(wp))


def _trunk(h, p_layers, attn_fn, W, n_tok_chunk, *, tm, tm_f):
    T = h.shape[0]
    depth = len(W)
    ks, vs, gs = [], [], []
    for i, lw in enumerate(W):
        w_mix, b_mix = lw["mix"][n_tok_chunk]
        q, kf, vf, kb, vb, og, vg, km = _inproj(h, lw["g_mix"], lw["w_in"], lw["q_scale"], w_mix, b_mix, tm=tm)
        om, od = attn_fn(i, lw, q, kf, vf, kb, vb, km)
        h1, hn, route = _post_a(h, om, od, og, lw["w_out"], lw["g_ffn"], lw["w_router"], tm=tm)
        if lw["moe"]:
            te, nu, slot_token, slot_scale, slot = _moe_dispatch(route, tm=tm_f)
            y = _ffn(te, nu, jnp.take(hn, slot_token, axis=0), lw["wg"], lw["wu"], lw["wd"], slot_scale, tm=tm_f)
            ys = [jnp.take(y, slot[:, j], axis=0) for j in range(TOP_K)]
        else:
            n_tiles = T // tm_f
            ys = [_ffn(jnp.zeros((n_tiles,), jnp.int32), jnp.full((1,), n_tiles, jnp.int32), hn,
                       lw["wg"], lw["wu"], lw["wd"], jnp.ones((T, 1), F32), tm=tm_f)]
        h = _post_b(h1, ys, p_layers[i].reshape(T, D_PLE), lw["g_ple"], lw["w_ple_gate"], lw["w_ple_proj"],
                    lw["g_final"], tm=tm, final=(i == depth - 1))
        ks.append(kf)
        vs.append(vf)
        gs.append(vg)
    return h, ks, vs, gs


def kernel(x_prompt, x_sample, cache_k, cache_v, page_table, p_prompt, p_sample, bias_table, norm_mix, w_in, w_out, lambda_q1, lambda_k1, lambda_q2, lambda_k2, diff_norm, gmlp_w_s, gmlp_b_s, norm_ffn, w_ffn_gate, w_ffn_up, w_ffn_down, w_router, w_exp_gate, w_exp_up, w_exp_down, norm_ple, w_ple_gate, w_ple_proj, norm_final):
    B, S, _ = x_prompt.shape
    nseq, n_tok, _ = x_sample.shape
    depth, n_pool = cache_k.shape[0], cache_k.shape[1]
    n_pages = page_table.shape[1]
    past_len = n_pages * PAGE_SIZE
    assert S % MOBA_BLOCK == 0 and S // MOBA_BLOCK <= LANES and past_len % MOBA_BLOCK == 0
    assert GMLP_CHUNK % n_tok == 0 and (nseq * n_tok) % MOBA_BLOCK == 0

    row2 = lambda a: a.astype(F32).reshape(1, -1)
    q_scale = jnp.asarray(np.concatenate([np.full(MOBA_W, HEAD_DIM ** -0.5, np.float32),
                                          np.full(DIFF_W, DIFF_QK ** -0.5, np.float32)])[None])
    tril = jnp.asarray(np.tril(np.ones((GMLP_CHUNK, GMLP_CHUNK), np.float32)))
    reps = GMLP_CHUNK // n_tok
    W = []
    for i in range(depth):
        ws = gmlp_w_s[i].astype(F32) * tril
        bs = gmlp_b_s[i].astype(F32)
        mix_prompt = (ws.astype(BF16), jnp.repeat(bs.T, HEAD_DIM, axis=1))
        w_small = ws[:, :n_tok, :n_tok]
        w_blockdiag = jnp.einsum("ab,gts->gatbs", jnp.eye(reps, dtype=F32), w_small).reshape(
            GMLP_GROUPS, GMLP_CHUNK, GMLP_CHUNK)
        mix_sample = (w_blockdiag.astype(BF16), jnp.tile(jnp.repeat(bs[:, :n_tok].T, HEAD_DIM, axis=1), (reps, 1)))
        lam = (jnp.exp(jnp.sum(lambda_q1[i].astype(F32) * lambda_k1[i].astype(F32)))
               - jnp.exp(jnp.sum(lambda_q2[i].astype(F32) * lambda_k2[i].astype(F32))) + _lambda_init(i))
        moe = i % 2 == 1
        j = i // 2
        W.append(dict(
            g_mix=row2(norm_mix[i]), w_in=_perm_w_in(w_in[i]), q_scale=q_scale,
            mix={"prompt": mix_prompt, "sample": mix_sample},
            w_out=w_out[i].astype(BF16), g_ffn=row2(norm_ffn[i]),
            w_router=_pad_router(w_router[j]) if moe else None, moe=moe,
            wg=(w_exp_gate[j] if moe else w_ffn_gate[j][None]).astype(BF16),
            wu=(w_exp_up[j] if moe else w_ffn_up[j][None]).astype(BF16),
            wd=(w_exp_down[j] if moe else w_ffn_down[j][None]).astype(BF16),
            g_ple=row2(norm_ple[i]), w_ple_gate=w_ple_gate[i].astype(BF16), w_ple_proj=w_ple_proj[i].astype(BF16),
            g_final=row2(norm_final), lam=lam.reshape(1, 1).astype(F32),
            g_diff=diff_norm[i].astype(F32), out_scale=1.0 - _lambda_init(i),
        ))

    bias_tiles = _prompt_bias_tiles(bias_table).reshape(KV_HEADS // 2, 2, N_BIAS_TILES, MOBA_BLOCK, MOBA_BLOCK)
    nblk = S // MOBA_BLOCK

    def prompt_attn(i, lw, q, kf, vf, kb, vb, km):
        q3, k3, v3 = (a.reshape(B, S, QKV_W) for a in (q, kb, vb))
        kmean = jnp.pad(km.reshape(B, nblk, QKV_W), ((0, 0), (0, LANES - nblk), (0, 0)))
        om = _moba_prompt(q3, k3, v3, kmean, bias_tiles)
        od = _diff_prompt(q3, k3, v3, bias_tiles, lw["lam"], lw["g_diff"].reshape(DIFF_HEADS // 2, 1, LANES),
                          lw["out_scale"])
        return om.reshape(B * S, MOBA_W), od.reshape(B * S, DIFF_W)

    y_p, k_p, v_p, _ = _trunk(x_prompt.reshape(B * S, D_MODEL), p_prompt, prompt_attn, W, "prompt",
                              tm=512, tm_f=512)

    cache_k2 = cache_k.reshape(depth * n_pool * PAGE_SIZE, QKV_W)
    cache_v2 = cache_v.reshape(depth * n_pool * PAGE_SIZE, QKV_W)
    sbias = _sample_bias(bias_table, past_len, n_tok)

    def sample_attn(i, lw, q, kf, vf, kb, vb, km):
        pad_new = lambda a: jnp.pad(a.reshape(nseq, n_tok, QKV_W), ((0, 0), (0, LANES - n_tok), (0, 0)))
        o = _sample_attn(i, n_pool, page_table, cache_k2, cache_v2, q.astype(F32).reshape(nseq, n_tok, QKV_W),
                         pad_new(kf), pad_new(vf), *sbias, lw["lam"], lw["g_diff"].reshape(1, DIFF_W),
                         lw["out_scale"])
        o = o.reshape(nseq * n_tok, QKV_W).astype(BF16)
        return o[:, :MOBA_W], o[:, MOBA_W:]

    y_s, k_s, v_s, g_s = _trunk(x_sample.reshape(nseq * n_tok, D_MODEL), p_sample, sample_attn, W, "sample",
                                tm=256, tm_f=128)

    heads = lambda rows, lead: jnp.stack(rows).reshape((depth,) + lead + (KV_HEADS, HEAD_DIM))
    return (y_p.reshape(B, S, D_MODEL), y_s.reshape(nseq, n_tok, D_MODEL),
            heads(k_p, (B, S)), heads(v_p, (B, S)), heads(k_s, (nseq, n_tok)), heads(v_s, (nseq, n_tok)),
            jnp.stack(g_s).reshape(depth, nseq, n_tok, GMLP_W))
```

```python
import functools
import math

import jax
import jax.numpy as jnp
import numpy as np
from jax import lax
from jax.experimental import pallas as pl
from jax.experimental.pallas import tpu as pltpu

F32 = jnp.float32
BF16 = jnp.bfloat16

D_MODEL = 1024
HEAD_DIM = 64
MOBA_HEADS = 6
DIFF_HEADS = 4
KV_HEADS = MOBA_HEADS + DIFF_HEADS
MOBA_W = MOBA_HEADS * HEAD_DIM
DIFF_W = DIFF_HEADS * HEAD_DIM
GMLP_GROUPS = 6
GMLP_W = GMLP_GROUPS * HEAD_DIM
QKV_W = MOBA_W + DIFF_W
DIFF_QK = HEAD_DIM // 2
MOBA_BLOCK = 256
MOBA_TOPK = 3
GMLP_CHUNK = 128
NUM_BUCKETS = 32
MAX_EXACT = 16
MAX_DISTANCE = 2048
D_FF = 2816
N_EXPERTS = 8
TOP_K = 2
D_PLE = 256
PAGE_SIZE = 128
RMS_EPS = 1e-6

LANES = 128
NEG = -1e30
FF_CHUNK = D_FF // 2
N_BIAS_TILES = 8
VMEM_LIMIT = 56 * 1024 * 1024


def _cparams(sem):
    return pltpu.CompilerParams(dimension_semantics=sem, vmem_limit_bytes=VMEM_LIMIT)


def _rms(x, g):
    return x * lax.rsqrt(jnp.mean(x * x, axis=-1, keepdims=True) + RMS_EPS) * g


def _bucket_np(dist):
    n = np.maximum(dist, 0)
    nf = np.maximum(n, MAX_EXACT).astype(np.float32)
    large = MAX_EXACT + (np.log(nf / np.float32(MAX_EXACT)) / np.float32(math.log(MAX_DISTANCE / MAX_EXACT))
                         * np.float32(NUM_BUCKETS - MAX_EXACT)).astype(np.int32)
    return np.where(n < MAX_EXACT, n, np.minimum(large, NUM_BUCKETS - 1)).astype(np.int32)


def _prompt_bias_tiles(bias_table):
    kj = np.arange(MOBA_BLOCK)[:, None]
    qi = np.arange(MOBA_BLOCK)[None, :]
    dist = np.stack([qi - kj + MOBA_BLOCK * d for d in range(N_BIAS_TILES)])
    assert _bucket_np(np.array([MOBA_BLOCK * (N_BIAS_TILES - 1) - MOBA_BLOCK + 1]))[0] == NUM_BUCKETS - 1
    tiles = jnp.take(bias_table.astype(F32), jnp.asarray(_bucket_np(dist)), axis=0)
    tiles = jnp.where(jnp.asarray(dist >= 0)[..., None], tiles, NEG)
    return jnp.transpose(tiles, (3, 0, 1, 2))


def _gmlp_tile(u, vg, ws_ref, bs_ref, og_ref, tm):
    vgb = vg.astype(BF16)
    first = lax.broadcasted_iota(jnp.int32, (1, LANES), 1) < HEAD_DIM
    for c in range(tm // GMLP_CHUNK):
        rows = slice(c * GMLP_CHUNK, (c + 1) * GMLP_CHUNK)
        for p in range(GMLP_GROUPS // 2):
            cols = slice(p * LANES, (p + 1) * LANES)
            vc = vgb[rows, cols]
            m0 = jnp.dot(ws_ref[2 * p], vc, preferred_element_type=F32)
            m1 = jnp.dot(ws_ref[2 * p + 1], vc, preferred_element_type=F32)
            mixed = jnp.where(first, m0, m1) + bs_ref[:, cols]
            og_ref[rows, cols] = (u[rows, cols] * mixed).astype(BF16)


def _inproj_prompt_kernel(x_ref, g_ref, wt_ref, w_ref, qs_ref, ws_ref, bs_ref,
                          qt_ref, kb_ref, kt_ref, vt_ref, vtb_ref, og_ref, km_ref, *, tm):
    xn = _rms(x_ref[...], g_ref[...]).astype(BF16)
    proj_t = lambda wt: lax.dot_general(wt, xn, (((1,), (1,)), ((), ())), preferred_element_type=F32)
    qt_ref[0] = (proj_t(wt_ref[0]) * qs_ref[...]).astype(BF16)
    kt_ref[0] = proj_t(wt_ref[1])
    vt = proj_t(wt_ref[2])
    vt_ref[0] = vt
    k = jnp.dot(xn, w_ref[:, :QKV_W], preferred_element_type=F32)
    kb_ref[...] = k.astype(BF16)
    for j in range(tm // MOBA_BLOCK):
        cols = slice(j * MOBA_BLOCK, (j + 1) * MOBA_BLOCK)
        vtb_ref[0, j] = vt[:, cols].astype(BF16)
        km_ref[j] = jnp.mean(k[cols], axis=0, keepdims=True)
    u = jax.nn.gelu(jnp.dot(xn, w_ref[:, QKV_W:QKV_W + GMLP_W], preferred_element_type=F32))
    vg = jax.nn.gelu(jnp.dot(xn, w_ref[:, QKV_W + GMLP_W:], preferred_element_type=F32))
    _gmlp_tile(u, vg, ws_ref, bs_ref, og_ref, tm)


def _inproj_prompt(x, g, w_t, w_kuv, q_scale_col, w_mix, b_mix, *, B, S, tm):
    assert S % tm == 0 and tm % MOBA_BLOCK == 0
    nj = S // tm
    nkm = tm // MOBA_BLOCK
    row = lambda w: pl.BlockSpec((tm, w), lambda b, j: (b * nj + j, 0))
    full = lambda a: pl.BlockSpec(a.shape, lambda b, j: (0,) * a.ndim)
    tcol = pl.BlockSpec((1, QKV_W, tm), lambda b, j: (b, 0, j))
    out_shape = (
        jax.ShapeDtypeStruct((B, QKV_W, S), BF16),
        jax.ShapeDtypeStruct((B * S, QKV_W), BF16),
        jax.ShapeDtypeStruct((B, QKV_W, S), F32),
        jax.ShapeDtypeStruct((B, QKV_W, S), F32),
        jax.ShapeDtypeStruct((B, S // MOBA_BLOCK, QKV_W, MOBA_BLOCK), BF16),
        jax.ShapeDtypeStruct((B * S, GMLP_W), BF16),
        jax.ShapeDtypeStruct((B * S // MOBA_BLOCK, 1, QKV_W), F32),
    )
    out_specs = (tcol, row(QKV_W), tcol, tcol,
                 pl.BlockSpec((1, nkm, QKV_W, MOBA_BLOCK), lambda b, j: (b, j, 0, 0)),
                 row(GMLP_W),
                 pl.BlockSpec((nkm, 1, QKV_W), lambda b, j: (b * nj + j, 0, 0)))
    return pl.pallas_call(
        functools.partial(_inproj_prompt_kernel, tm=tm),
        grid=(B, nj),
        in_specs=[row(D_MODEL), full(g), full(w_t), full(w_kuv), full(q_scale_col), full(w_mix), full(b_mix)],
        out_specs=out_specs,
        out_shape=out_shape,
        compiler_params=_cparams(("parallel", "parallel")),
        name="inproj_prompt",
    )(x, g, w_t, w_kuv, q_scale_col, w_mix, b_mix)


def _inproj_sample_kernel(x_ref, g_ref, w_ref, qs_ref, ws_ref, bs_ref, q_ref, kf_ref, vf_ref, og_ref, vg_ref, *, tm):
    xn = _rms(x_ref[...], g_ref[...]).astype(BF16)
    c1, c2, c3, c4 = QKV_W, 2 * QKV_W, 3 * QKV_W, 3 * QKV_W + GMLP_W
    q_ref[...] = jnp.dot(xn, w_ref[:, :c1], preferred_element_type=F32) * qs_ref[...]
    kf_ref[...] = jnp.dot(xn, w_ref[:, c1:c2], preferred_element_type=F32)
    vf_ref[...] = jnp.dot(xn, w_ref[:, c2:c3], preferred_element_type=F32)
    u = jax.nn.gelu(jnp.dot(xn, w_ref[:, c3:c4], preferred_element_type=F32))
    vg = jax.nn.gelu(jnp.dot(xn, w_ref[:, c4:], preferred_element_type=F32))
    vg_ref[...] = vg
    _gmlp_tile(u, vg, ws_ref, bs_ref, og_ref, tm)


def _inproj_sample(x, g, w_perm, q_scale, w_mix, b_mix, *, tm):
    T = x.shape[0]
    assert T % tm == 0 and tm % GMLP_CHUNK == 0
    row = lambda w: pl.BlockSpec((tm, w), lambda i: (i, 0))
    full = lambda a: pl.BlockSpec(a.shape, lambda i: (0,) * a.ndim)
    out_shape = (
        jax.ShapeDtypeStruct((T, QKV_W), F32),
        jax.ShapeDtypeStruct((T, QKV_W), F32),
        jax.ShapeDtypeStruct((T, QKV_W), F32),
        jax.ShapeDtypeStruct((T, GMLP_W), BF16),
        jax.ShapeDtypeStruct((T, GMLP_W), F32),
    )
    return pl.pallas_call(
        functools.partial(_inproj_sample_kernel, tm=tm),
        grid=(T // tm,),
        in_specs=[row(D_MODEL), full(g), full(w_perm), full(q_scale), full(w_mix), full(b_mix)],
        out_specs=(row(QKV_W), row(QKV_W), row(QKV_W), row(GMLP_W), row(GMLP_W)),
        out_shape=out_shape,
        compiler_params=_cparams(("parallel",)),
        name="inproj_sample",
    )(x, g, w_perm, q_scale, w_mix, b_mix)


def _masked_queries(qt, group_width, n_maps):
    sub = lax.broadcasted_iota(jnp.int32, (LANES, 1), 0)
    zero = jnp.zeros_like(qt)
    return jnp.concatenate([jnp.where(sub // group_width == j, qt, zero) for j in range(n_maps)], axis=1)


def _softmax_first(s, vt, m_ref, l_ref, acc_ref):
    m = jnp.max(s, axis=0, keepdims=True)
    p = jnp.exp(s - m)
    m_ref[...] = m
    l_ref[...] = jnp.sum(p, axis=0, keepdims=True)
    acc_ref[...] = jnp.dot(vt, p.astype(BF16), preferred_element_type=F32)


def _softmax_update(s, vt, m_ref, l_ref, acc_ref):
    m_old = m_ref[...]
    m = jnp.maximum(m_old, jnp.max(s, axis=0, keepdims=True))
    alpha = jnp.exp(m_old - m)
    p = jnp.exp(s - m)
    m_ref[...] = m
    l_ref[...] = alpha * l_ref[...] + jnp.sum(p, axis=0, keepdims=True)
    acc_ref[...] = alpha * acc_ref[...] + jnp.dot(vt, p.astype(BF16), preferred_element_type=F32)


def _qk(qm, k):
    return lax.dot_general(qm, k, (((1,), (1,)), ((), ())), preferred_element_type=F32)


def _split3(x):
    hi = x.astype(BF16)
    r = x - hi.astype(F32)
    mid = r.astype(BF16)
    lo = (r - mid.astype(F32)).astype(BF16)
    return hi, mid, lo


def _top3_mask(gate, pos, axis):
    sel = jnp.zeros(gate.shape, F32)
    for _ in range(MOBA_TOPK):
        m = jnp.max(gate, axis=axis, keepdims=True)
        idx = jnp.min(jnp.where(gate == m, pos, gate.shape[axis]), axis=axis, keepdims=True)
        hit = pos == idx
        sel = jnp.where(hit, 1.0, sel)
        gate = jnp.where(hit, -jnp.inf, gate)
    return sel


def _moba_prompt_kernel(qt_ref, k_ref, vt_ref, km_ref, bias_ref, o_ref, qw_ref, m_ref, l_ref, acc_ref, sel_ref):
    i = pl.program_id(2)
    blk = MOBA_BLOCK
    dot = lambda a, b: jnp.dot(a, b, preferred_element_type=F32)
    qw_ref[...] = _masked_queries(qt_ref[0], HEAD_DIM, 2)
    qw = qw_ref[...]
    blk_id = lax.broadcasted_iota(jnp.int32, sel_ref.shape, 0)
    km_hi, km_mid, km_lo = _split3(km_ref[0])
    gate = dot(km_hi, qw) + dot(km_mid, qw) + dot(km_lo, qw)
    gate = jnp.where(blk_id < i, gate, -jnp.inf)
    sel_ref[...] = jnp.where(blk_id < i, _top3_mask(gate, blk_id, 0), 0.0)
    bias = lambda d: jnp.concatenate([bias_ref[0, 0, d], bias_ref[0, 1, d]], axis=1)
    k_own = k_ref[0, pl.ds(pl.multiple_of(i * blk, blk), blk), :]
    _softmax_first(dot(k_own, qw) + bias(0), vt_ref[0, i], m_ref, l_ref, acc_ref)

    def body(n, carry):
        kb = k_ref[0, pl.ds(pl.multiple_of(n * blk, blk), blk), :]
        s = dot(kb, qw_ref[...]) + bias(jnp.minimum(i - n, N_BIAS_TILES - 1))
        picked = sel_ref[pl.ds(n, 1), :]
        _softmax_update(jnp.where(picked > 0.5, s, NEG), vt_ref[0, n], m_ref, l_ref, acc_ref)
        return carry

    lax.fori_loop(0, i, body, 0)
    o = acc_ref[...] / l_ref[...]
    sub = lax.broadcasted_iota(jnp.int32, (LANES, 1), 0)
    o_ref[0] = jnp.where(sub < HEAD_DIM, o[:, :blk], o[:, blk:]).T.astype(BF16)


def _prompt_attn_specs(S, nbp, off):
    blk = MOBA_BLOCK
    return dict(
        qt=pl.BlockSpec((1, LANES, blk), lambda b, p, i: (b, p + off, i)),
        k=pl.BlockSpec((1, S, LANES), lambda b, p, i: (b, 0, p + off)),
        vt=pl.BlockSpec((1, S // blk, LANES, blk), lambda b, p, i: (b, 0, p + off, 0)),
        km=pl.BlockSpec((1, nbp, LANES), lambda b, p, i: (b, 0, p + off)),
        bias=pl.BlockSpec((1, 2, N_BIAS_TILES, blk, blk), lambda b, p, i: (p + off, 0, 0, 0, 0)),
        out=pl.BlockSpec((1, blk, LANES), lambda b, p, i: (b, i, p)),
    )


def _moba_prompt(qt, k, vtb, kmean, bias_tiles):
    B, S, _ = k.shape
    blk = MOBA_BLOCK
    nbp = kmean.shape[1]
    sp = _prompt_attn_specs(S, nbp, 0)
    return pl.pallas_call(
        _moba_prompt_kernel,
        grid=(B, MOBA_HEADS // 2, S // blk),
        in_specs=[sp["qt"], sp["k"], sp["vt"], sp["km"], sp["bias"]],
        out_specs=sp["out"],
        out_shape=jax.ShapeDtypeStruct((B, S, MOBA_W), BF16),
        scratch_shapes=[pltpu.VMEM((LANES, 2 * blk), BF16), pltpu.VMEM((1, 2 * blk), F32), pltpu.VMEM((1, 2 * blk), F32),
                        pltpu.VMEM((LANES, 2 * blk), F32), pltpu.VMEM((nbp, 2 * blk), F32)],
        compiler_params=_cparams(("parallel", "parallel", "arbitrary")),
        name="moba_prompt",
    )(qt, k, vtb, kmean, bias_tiles)


def _diff_prompt_kernel(lam_ref, qt_ref, k_ref, vt_ref, bias_ref, g_ref, o_ref, qw_ref, m_ref, l_ref, acc_ref,
                        *, out_scale):
    i = pl.program_id(2)
    blk = MOBA_BLOCK
    dot = lambda a, b: jnp.dot(a, b, preferred_element_type=F32)
    qw_ref[...] = _masked_queries(qt_ref[0], DIFF_QK, 4)
    bias = lambda d: jnp.concatenate([bias_ref[0, j // 2, d] for j in range(4)], axis=1)
    k_own = k_ref[0, pl.ds(pl.multiple_of(i * blk, blk), blk), :]
    _softmax_first(dot(k_own, qw_ref[...]) + bias(0), vt_ref[0, i], m_ref, l_ref, acc_ref)

    def body(n, carry):
        kb = k_ref[0, pl.ds(pl.multiple_of(n * blk, blk), blk), :]
        s = dot(kb, qw_ref[...]) + bias(jnp.minimum(i - n, N_BIAS_TILES - 1))
        _softmax_update(s, vt_ref[0, n], m_ref, l_ref, acc_ref)
        return carry

    lax.fori_loop(0, i, body, 0)
    lam = lam_ref[0, 0]
    sub = lax.broadcasted_iota(jnp.int32, (LANES, 1), 0)
    o = acc_ref[...] / l_ref[...]
    outs = []
    for h in range(2):
        oh = o[:, (2 * h) * blk:(2 * h + 1) * blk] - lam * o[:, (2 * h + 1) * blk:(2 * h + 2) * blk]
        ms = jnp.sum(jnp.where(sub // HEAD_DIM == h, oh * oh, 0.0), axis=0, keepdims=True) * (1.0 / HEAD_DIM)
        outs.append(oh * lax.rsqrt(ms + RMS_EPS))
    ot = jnp.where(sub < HEAD_DIM, outs[0], outs[1]) * g_ref[0] * out_scale
    o_ref[0] = ot.T.astype(BF16)


def _diff_prompt(qt, k, vtb, bias_tiles, lam, g_col, out_scale):
    B, S, _ = k.shape
    blk = MOBA_BLOCK
    sp = _prompt_attn_specs(S, 0, MOBA_HEADS // 2)
    return pl.pallas_call(
        functools.partial(_diff_prompt_kernel, out_scale=out_scale),
        grid=(B, DIFF_HEADS // 2, S // blk),
        in_specs=[pl.BlockSpec(memory_space=pltpu.SMEM), sp["qt"], sp["k"], sp["vt"], sp["bias"],
                  pl.BlockSpec((1, LANES, 1), lambda b, p, i: (p, 0, 0))],
        out_specs=sp["out"],
        out_shape=jax.ShapeDtypeStruct((B, S, DIFF_W), BF16),
        scratch_shapes=[pltpu.VMEM((LANES, 4 * blk), BF16), pltpu.VMEM((1, 4 * blk), F32), pltpu.VMEM((1, 4 * blk), F32),
                        pltpu.VMEM((LANES, 4 * blk), F32)],
        compiler_params=_cparams(("parallel", "parallel", "arbitrary")),
        name="diff_prompt",
    )(lam, qt, k, vtb, bias_tiles, g_col)


def _post_a_kernel(x_ref, om_ref, od_ref, og_ref, wo_ref, g_ref, *rest):
    h = x_ref[...]
    h = h + jnp.dot(om_ref[...], wo_ref[0:MOBA_W, :], preferred_element_type=F32)
    h = h + jnp.dot(od_ref[...], wo_ref[MOBA_W:QKV_W, :], preferred_element_type=F32)
    h = h + jnp.dot(og_ref[...], wo_ref[QKV_W:, :], preferred_element_type=F32)
    if len(rest) == 2:
        h_ref, hn_ref = rest
    else:
        wr_ref, h_ref, hn_ref, route_ref = rest
    h_ref[...] = h
    hn = _rms(h, g_ref[...])
    hn_ref[...] = hn.astype(BF16)
    if len(rest) == 2:
        return
    a_hi, a_mid, _ = _split3(hn)
    dot = lambda a, b: jnp.dot(a, b, preferred_element_type=F32)
    logits = dot(a_hi, wr_ref[0]) + (dot(a_hi, wr_ref[1]) + dot(a_mid, wr_ref[0]))
    col = lax.broadcasted_iota(jnp.int32, logits.shape, 1)
    logits = jnp.where(col < N_EXPERTS, logits, -jnp.inf)
    m1 = jnp.max(logits, axis=1, keepdims=True)
    i1 = jnp.min(jnp.where(logits == m1, col, LANES), axis=1, keepdims=True)
    rest = jnp.where(col == i1, -jnp.inf, logits)
    m2 = jnp.max(rest, axis=1, keepdims=True)
    i2 = jnp.min(jnp.where(rest == m2, col, LANES), axis=1, keepdims=True)
    e = jnp.exp(m2 - m1)
    w1 = 1.0 / (1.0 + e)
    w2 = e / (1.0 + e)
    route = jnp.where(col == 0, i1.astype(F32), 0.0)
    route = jnp.where(col == 1, i2.astype(F32), route)
    route = jnp.where(col == 2, w1, route)
    route = jnp.where(col == 3, w2, route)
    route_ref[...] = route


def _post_a(x, om, od, og, w_out, g, w_router3, *, tm):
    T = x.shape[0]
    row = lambda w: pl.BlockSpec((tm, w), lambda i: (i, 0))
    full = lambda a: pl.BlockSpec(a.shape, lambda i: (0,) * a.ndim)
    routed = w_router3 is not None
    outs = pl.pallas_call(
        _post_a_kernel,
        grid=(T // tm,),
        in_specs=[row(D_MODEL), row(MOBA_W), row(DIFF_W), row(GMLP_W), full(w_out), full(g)]
                 + ([full(w_router3)] if routed else []),
        out_specs=(row(D_MODEL), row(D_MODEL)) + ((row(LANES),) if routed else ()),
        out_shape=(jax.ShapeDtypeStruct((T, D_MODEL), F32), jax.ShapeDtypeStruct((T, D_MODEL), BF16))
                  + ((jax.ShapeDtypeStruct((T, LANES), F32),) if routed else ()),
        compiler_params=_cparams(("parallel",)),
        name="post_a",
    )(x, om, od, og, w_out, g, *([w_router3] if routed else []))
    return outs if routed else (*outs, None)


def _ffn_kernel(te_ref, nu_ref, x_ref, wg_ref, wu_ref, wd_ref, sc_ref, o_ref, acc_ref):
    t = pl.program_id(0)
    f = pl.program_id(1)

    @pl.when(t < nu_ref[0])
    def _():
        x = x_ref[...]
        g = jnp.dot(x, wg_ref[0], preferred_element_type=F32)
        u = jnp.dot(x, wu_ref[0], preferred_element_type=F32)
        a = (g * jax.nn.sigmoid(g) * u).astype(BF16)
        part = jnp.dot(a, wd_ref[0], preferred_element_type=F32)

        @pl.when(f == 0)
        def _():
            acc_ref[...] = part

        @pl.when(f == 1)
        def _():
            o_ref[...] = (acc_ref[...] + part) * sc_ref[...]

    @pl.when(jnp.logical_and(t >= nu_ref[0], f == 1))
    def _():
        o_ref[...] = jnp.zeros_like(o_ref)


def _ffn(tile_expert, n_used, x, wg, wu, wd, scale, *, tm):
    N = x.shape[0]
    n_tiles = N // tm
    chunk = lambda t, f: jnp.where(t % 2 == 0, f, 1 - f)
    return pl.pallas_call(
        _ffn_kernel,
        grid_spec=pltpu.PrefetchScalarGridSpec(
            num_scalar_prefetch=2,
            grid=(n_tiles, 2),
            in_specs=[
                pl.BlockSpec((tm, D_MODEL), lambda t, f, te, nu: (t, 0)),
                pl.BlockSpec((1, D_MODEL, FF_CHUNK), lambda t, f, te, nu: (te[t], 0, chunk(t, f))),
                pl.BlockSpec((1, D_MODEL, FF_CHUNK), lambda t, f, te, nu: (te[t], 0, chunk(t, f))),
                pl.BlockSpec((1, FF_CHUNK, D_MODEL), lambda t, f, te, nu: (te[t], chunk(t, f), 0)),
                pl.BlockSpec((tm, 1), lambda t, f, te, nu: (t, 0)),
            ],
            out_specs=pl.BlockSpec((tm, D_MODEL), lambda t, f, te, nu: (t, 0)),
            scratch_shapes=[pltpu.VMEM((tm, D_MODEL), F32)],
        ),
        out_shape=jax.ShapeDtypeStruct((N, D_MODEL), F32),
        compiler_params=_cparams(("arbitrary", "arbitrary")),
        name="ffn",
    )(tile_expert, n_used, x, wg, wu, wd, scale)


def _moe_dispatch(route, *, tm):
    T = route.shape[0]
    n_assign = T * TOP_K
    n_tiles = -(-n_assign // tm) + N_EXPERTS
    experts = route[:, :TOP_K].astype(jnp.int32).reshape(n_assign)
    weights = route[:, TOP_K:2 * TOP_K].reshape(n_assign)
    onehot = (experts[:, None] == jnp.arange(N_EXPERTS)[None, :]).astype(jnp.int32)
    rank = jnp.sum((jnp.cumsum(onehot, axis=0) - onehot) * onehot, axis=1)
    counts = jnp.sum(onehot, axis=0)
    tiles_per = (counts + tm - 1) // tm
    tile_end = jnp.cumsum(tiles_per)
    slot = ((tile_end - tiles_per) * tm)[experts] + rank
    tile_expert = jnp.minimum(jnp.searchsorted(tile_end, jnp.arange(n_tiles), side="right"),
                              N_EXPERTS - 1).astype(jnp.int32)
    n_used = tile_end[-1:].astype(jnp.int32)
    token = jnp.arange(n_assign, dtype=jnp.int32) // TOP_K
    slot_token = jnp.zeros((n_tiles * tm,), jnp.int32).at[slot].set(token)
    slot_scale = jnp.zeros((n_tiles * tm,), F32).at[slot].set(weights)
    return tile_expert, n_used, slot_token, slot_scale[:, None], slot.reshape(T, TOP_K)


def _post_b_kernel(*refs, n_y, final):
    h_ref = refs[0]
    y_refs = refs[1:1 + n_y]
    p_ref, gp_ref, wgate_ref, wproj_ref, gf_ref, o_ref = refs[1 + n_y:]
    h = h_ref[...]
    for y_ref in y_refs:
        h = h + y_ref[...]
    gate = jax.nn.sigmoid(jnp.dot(_rms(h, gp_ref[...]).astype(BF16), wgate_ref[...], preferred_element_type=F32))
    h = h + gate * jnp.dot(p_ref[...].astype(BF16), wproj_ref[...], preferred_element_type=F32)
    o_ref[...] = _rms(h, gf_ref[...]) if final else h


def _post_b(h, ys, p, g_ple, w_gate, w_proj, g_final, *, tm, final):
    T = h.shape[0]
    row = lambda w: pl.BlockSpec((tm, w), lambda i: (i, 0))
    full = lambda a: pl.BlockSpec(a.shape, lambda i: (0,) * a.ndim)
    return pl.pallas_call(
        functools.partial(_post_b_kernel, n_y=len(ys), final=final),
        grid=(T // tm,),
        in_specs=[row(D_MODEL)] + [row(D_MODEL)] * len(ys)
                 + [row(D_PLE), full(g_ple), full(w_gate), full(w_proj), full(g_final)],
        out_specs=row(D_MODEL),
        out_shape=jax.ShapeDtypeStruct((T, D_MODEL), F32),
        compiler_params=_cparams(("parallel",)),
        name="post_b",
    )(h, *ys, p, g_ple, w_gate, w_proj, g_final)


PAGES_PER_STEP = 8
DIFF_MAPS = 2 * DIFF_HEADS


def _sample_attn_kernel(pt_ref, lam_ref, *refs, n_tok, n_steps, out_scale):
    P = PAGES_PER_STEP
    k_refs = refs[:P]
    v_refs = refs[P:2 * P]
    (q_ref, kn_ref, vn_ref, bm_ref, bd_ref, bmo_ref, bdo_ref, g_ref, o_ref,
     qm_ref, qd_ref, gate_ref, mb_ref, lb_ref, accb_ref, m_ref, l_ref, acc_ref) = refs[2 * P:]
    s = pl.program_id(1)
    rm = MOBA_HEADS * n_tok
    rd = DIFF_MAPS * n_tok
    lane = lax.broadcasted_iota(jnp.int32, (1, LANES), 1)

    @pl.when(s == 0)
    def _():
        q = q_ref[0]
        qm = jnp.concatenate([q[:, :MOBA_W]] * MOBA_HEADS, axis=0)
        r = lax.broadcasted_iota(jnp.int32, qm.shape, 0) // n_tok
        c = lax.broadcasted_iota(jnp.int32, qm.shape, 1) // HEAD_DIM
        qm_ref[...] = jnp.where(r == c, qm, 0.0).astype(BF16)
        qd = jnp.concatenate([q[:, MOBA_W:]] * DIFF_MAPS, axis=0)
        r = lax.broadcasted_iota(jnp.int32, qd.shape, 0) // n_tok
        c = lax.broadcasted_iota(jnp.int32, qd.shape, 1) // DIFF_QK
        qd_ref[...] = jnp.where(r == c, qd, 0.0).astype(BF16)
        gate_ref[...] = jnp.full(gate_ref.shape, -jnp.inf, F32)
        mb_ref[...] = jnp.full(mb_ref.shape, NEG, F32)
        lb_ref[...] = jnp.zeros(lb_ref.shape, F32)
        m_ref[...] = jnp.full(m_ref.shape, NEG, F32)
        l_ref[...] = jnp.zeros(l_ref.shape, F32)
        acc_ref[...] = jnp.zeros(acc_ref.shape, F32)

    def scores(q, k, transposed):
        return jnp.dot(q, k, preferred_element_type=F32) if transposed else _qk(q, k)

    def weighted(p, v, transposed):
        return _qk(p, v) if transposed else jnp.dot(p, v, preferred_element_type=F32)

    def diff_update(kd, vd, bias, transposed):
        sd = scores(qd_ref[...], kd, transposed) + bias
        m_old = m_ref[...]
        m = jnp.maximum(m_old, jnp.max(sd, axis=1, keepdims=True))
        alpha = jnp.exp(m_old - m)
        p = jnp.exp(sd - m)
        m_ref[...] = m
        l_ref[...] = alpha * l_ref[...] + jnp.sum(p, axis=1, keepdims=True)
        acc_ref[...] = alpha * acc_ref[...] + weighted(p.astype(BF16), vd, transposed)

    def moba_partial(km, vm, bias, transposed):
        raw = scores(qm_ref[...], km, transposed)
        sm = raw + bias
        m = jnp.max(sm, axis=1, keepdims=True)
        p = jnp.exp(sm - m)
        return raw, m, jnp.sum(p, axis=1, keepdims=True), weighted(p.astype(BF16), vm, transposed)

    page = lambda ref: ref[0, 0].reshape(QKV_W, PAGE_SIZE)
    for bi in range(P // 2):
        n = s * (P // 2) + bi
        kt = jnp.concatenate([page(k_refs[2 * bi]), page(k_refs[2 * bi + 1])], axis=1).astype(BF16)
        vt = jnp.concatenate([page(v_refs[2 * bi]), page(v_refs[2 * bi + 1])], axis=1).astype(BF16)
        raw, m, l, acc = moba_partial(kt[:MOBA_W], vt[:MOBA_W], bm_ref[n], True)
        gcol = jnp.mean(raw, axis=1, keepdims=True)
        gate_ref[...] = jnp.where(lane == n, gcol, gate_ref[...])
        mb_ref[...] = jnp.where(lane == n, m, mb_ref[...])
        lb_ref[...] = jnp.where(lane == n, l, lb_ref[...])
        accb_ref[n] = acc
        diff_update(kt[MOBA_W:], vt[MOBA_W:], bd_ref[n], True)

    @pl.when(s == n_steps - 1)
    def _():
        kn = kn_ref[0].astype(BF16)
        vn = vn_ref[0].astype(BF16)
        diff_update(kn[:, MOBA_W:], vn[:, MOBA_W:], bdo_ref[...], False)
        _, m_own, l_own, acc_own = moba_partial(kn[:, :MOBA_W], vn[:, :MOBA_W], bmo_ref[...], False)
        col = lax.broadcasted_iota(jnp.int32, (rm, LANES), 1)
        sel = _top3_mask(gate_ref[...], col, 1)
        sel = jnp.where(gate_ref[...] > -jnp.inf, sel, 0.0)
        mb = mb_ref[...]
        m_fin = jnp.maximum(m_own, jnp.max(jnp.where(sel > 0.5, mb, NEG), axis=1, keepdims=True))
        w = sel * jnp.exp(jnp.minimum(mb - m_fin, 0.0))
        w_own = jnp.exp(m_own - m_fin)
        l_fin = w_own * l_own + jnp.sum(w * lb_ref[...], axis=1, keepdims=True)

        def merge(nb, acc):
            wn = jnp.sum(jnp.where(col == nb, w, 0.0), axis=1, keepdims=True)
            return acc + wn * accb_ref[nb]

        acc_fin = lax.fori_loop(0, n_steps * (P // 2), merge, w_own * acc_own)
        om_rows = acc_fin / l_fin
        lane_m = lax.broadcasted_iota(jnp.int32, (n_tok, MOBA_W), 1) // HEAD_DIM
        om = jnp.zeros((n_tok, MOBA_W), F32)
        for h in range(MOBA_HEADS):
            om = jnp.where(lane_m == h, om_rows[h * n_tok:(h + 1) * n_tok], om)

        od_rows = acc_ref[...] / l_ref[...]
        lane_d = lax.broadcasted_iota(jnp.int32, (n_tok, DIFF_W), 1) // HEAD_DIM
        lam = lam_ref[0, 0]
        od = jnp.zeros((n_tok, DIFF_W), F32)
        for h in range(DIFF_HEADS):
            a1 = od_rows[(2 * h) * n_tok:(2 * h + 1) * n_tok]
            a2 = od_rows[(2 * h + 1) * n_tok:(2 * h + 2) * n_tok]
            od = jnp.where(lane_d == h, a1 - lam * a2, od)
        normed = jnp.zeros_like(od)
        for h in range(DIFF_HEADS):
            ms = jnp.sum(jnp.where(lane_d == h, od * od, 0.0), axis=1, keepdims=True) * (1.0 / HEAD_DIM)
            normed = jnp.where(lane_d == h, od * lax.rsqrt(ms + RMS_EPS), normed)
        o_ref[0, :, :MOBA_W] = om
        o_ref[0, :, MOBA_W:] = normed * g_ref[...] * out_scale


def _sample_attn(layer, page_table, cache_kt, cache_vt, q, k_new, v_new,
                 bias_m, bias_d, bias_mo, bias_do, lam, g, out_scale):
    nseq, n_pages = page_table.shape
    n_tok = q.shape[1]
    P = PAGES_PER_STEP
    assert n_pages % P == 0 and n_pages // 2 <= LANES and n_tok % 8 == 0
    n_steps = n_pages // P
    nb = n_pages // 2
    rm, rd = MOBA_HEADS * n_tok, DIFF_MAPS * n_tok
    page = lambda j: pl.BlockSpec((1, 1, KV_HEADS, HEAD_DIM, PAGE_SIZE),
                                  lambda b, s, pt, j=j: (layer, pt[b, s * P + j], 0, 0, 0))
    const = lambda a: pl.BlockSpec(a.shape, lambda b, s, pt: (0,) * a.ndim)
    per_seq = lambda a: pl.BlockSpec((1,) + a.shape[1:], lambda b, s, pt: (b,) + (0,) * (a.ndim - 1))
    return pl.pallas_call(
        functools.partial(_sample_attn_kernel, n_tok=n_tok, n_steps=n_steps, out_scale=out_scale),
        grid_spec=pltpu.PrefetchScalarGridSpec(
            num_scalar_prefetch=1,
            grid=(nseq, n_steps),
            in_specs=[pl.BlockSpec(memory_space=pltpu.SMEM)]
                     + [page(j) for j in range(P)] + [page(j) for j in range(P)]
                     + [per_seq(q), per_seq(k_new), per_seq(v_new),
                        const(bias_m), const(bias_d), const(bias_mo), const(bias_do), const(g)],
            out_specs=pl.BlockSpec((1, n_tok, QKV_W), lambda b, s, pt: (b, 0, 0)),
            scratch_shapes=[
                pltpu.VMEM((rm, MOBA_W), BF16), pltpu.VMEM((rd, DIFF_W), BF16),
                pltpu.VMEM((rm, LANES), F32), pltpu.VMEM((rm, LANES), F32), pltpu.VMEM((rm, LANES), F32),
                pltpu.VMEM((nb, rm, MOBA_W), F32),
                pltpu.VMEM((rd, 1), F32), pltpu.VMEM((rd, 1), F32), pltpu.VMEM((rd, DIFF_W), F32),
            ],
        ),
        out_shape=jax.ShapeDtypeStruct((nseq, n_tok, QKV_W), F32),
        compiler_params=_cparams(("parallel", "arbitrary")),
        name="sample_attn",
    )(page_table, lam, *([cache_kt] * P), *([cache_vt] * P), q, k_new, v_new,
      bias_m, bias_d, bias_mo, bias_do, g)


def _sample_bias(bias_table, past_len, n_tok):
    t = np.arange(n_tok)
    d_past = _bucket_np(past_len + t[:, None] - np.arange(past_len)[None, :])
    d_own = t[:, None] - np.arange(LANES)[None, :]
    valid = (d_own >= 0) & (np.arange(LANES)[None, :] < n_tok)
    tab = bias_table.astype(F32).T
    past = tab[:, jnp.asarray(d_past)]
    own = jnp.where(jnp.asarray(valid)[None], tab[:, jnp.asarray(_bucket_np(d_own))], NEG)
    stack_m = lambda a: a[:MOBA_HEADS].reshape(MOBA_HEADS * n_tok, -1)
    stack_d = lambda a: jnp.repeat(a[MOBA_HEADS:], 2, axis=0).reshape(DIFF_MAPS * n_tok, -1)
    blocks = lambda a: jnp.transpose(a.reshape(a.shape[0], past_len // MOBA_BLOCK, MOBA_BLOCK), (1, 0, 2))
    return blocks(stack_m(past)), blocks(stack_d(past)), stack_m(own), stack_d(own)


def _lambda_init(i):
    return 0.8 - 0.6 * math.exp(-0.3 * i)


def _perm_w_in(w):
    sizes = (MOBA_W, MOBA_W, MOBA_W, DIFF_W, DIFF_W, DIFF_W, GMLP_W, GMLP_W)
    o = np.concatenate([[0], np.cumsum(sizes)])
    qm, km, vm, qd, kd, vd, u, vg = [w[:, o[j]:o[j + 1]] for j in range(8)]
    return jnp.concatenate([qm, qd, km, kd, vm, vd, u, vg], axis=1).astype(BF16)


def _pad_router(w):
    wp = jnp.pad(w.astype(F32), ((0, 0), (0, LANES - N_EXPERTS)))
    return jnp.stack(_split3(wp))


TM_PROMPT, TM_SAMPLE = 512, 256
TMF_PROMPT, TMF_SAMPLE = 512, 128


def _trunk(h, p_layers, mix_fn, W, *, tm, tm_f):
    T = h.shape[0]
    depth = len(W)
    ks, vs, gs = [], [], []
    for i, lw in enumerate(W):
        om, od, og, k_out, v_out, vg = mix_fn(i, lw, h)
        h1, hn, route = _post_a(h, om, od, og, lw["w_out"], lw["g_ffn"], lw["w_router"], tm=tm)
        if lw["moe"]:
            te, nu, slot_token, slot_scale, slot = _moe_dispatch(route, tm=tm_f)
            y = _ffn(te, nu, jnp.take(hn, slot_token, axis=0), lw["wg"], lw["wu"], lw["wd"], slot_scale, tm=tm_f)
            ys = [jnp.take(y, slot[:, j], axis=0) for j in range(TOP_K)]
        else:
            n_tiles = T // tm_f
            ys = [_ffn(jnp.zeros((n_tiles,), jnp.int32), jnp.full((1,), n_tiles, jnp.int32), hn,
                       lw["wg"], lw["wu"], lw["wd"], jnp.ones((T, 1), F32), tm=tm_f)]
        h = _post_b(h1, ys, p_layers[i].reshape(T, D_PLE), lw["g_ple"], lw["w_ple_gate"], lw["w_ple_proj"],
                    lw["g_final"], tm=tm, final=(i == depth - 1))
        ks.append(k_out)
        vs.append(v_out)
        gs.append(vg)
    return h, ks, vs, gs


def kernel(x_prompt, x_sample, cache_k, cache_v, page_table, p_prompt, p_sample, bias_table, norm_mix, w_in, w_out, lambda_q1, lambda_k1, lambda_q2, lambda_k2, diff_norm, gmlp_w_s, gmlp_b_s, norm_ffn, w_ffn_gate, w_ffn_up, w_ffn_down, w_router, w_exp_gate, w_exp_up, w_exp_down, norm_ple, w_ple_gate, w_ple_proj, norm_final):
    B, S, _ = x_prompt.shape
    nseq, n_tok, _ = x_sample.shape
    depth = cache_k.shape[0]
    n_pages = page_table.shape[1]
    past_len = n_pages * PAGE_SIZE
    nblk = S // MOBA_BLOCK
    assert S % TM_PROMPT == 0 and nblk <= LANES and past_len % MOBA_BLOCK == 0
    assert GMLP_CHUNK % n_tok == 0 and (nseq * n_tok) % TM_SAMPLE == 0

    row2 = lambda a: a.astype(F32).reshape(1, -1)
    q_scale = jnp.asarray(np.concatenate([np.full(MOBA_W, HEAD_DIM ** -0.5, np.float32),
                                          np.full(DIFF_W, DIFF_QK ** -0.5, np.float32)])[None])
    tril = jnp.asarray(np.tril(np.ones((GMLP_CHUNK, GMLP_CHUNK), np.float32)))
    reps = GMLP_CHUNK // n_tok
    W = []
    for i in range(depth):
        ws = gmlp_w_s[i].astype(F32) * tril
        bs = gmlp_b_s[i].astype(F32)
        mix_prompt = (ws.astype(BF16), jnp.repeat(bs.T, HEAD_DIM, axis=1))
        w_small = ws[:, :n_tok, :n_tok]
        w_blockdiag = jnp.einsum("ab,gts->gatbs", jnp.eye(reps, dtype=F32), w_small).reshape(
            GMLP_GROUPS, GMLP_CHUNK, GMLP_CHUNK)
        mix_sample = (w_blockdiag.astype(BF16), jnp.tile(jnp.repeat(bs[:, :n_tok].T, HEAD_DIM, axis=1), (reps, 1)))
        lam = (jnp.exp(jnp.sum(lambda_q1[i].astype(F32) * lambda_k1[i].astype(F32)))
               - jnp.exp(jnp.sum(lambda_q2[i].astype(F32) * lambda_k2[i].astype(F32))) + _lambda_init(i))
        moe = i % 2 == 1
        j = i // 2
        w_perm = _perm_w_in(w_in[i])
        W.append(dict(
            g_mix=row2(norm_mix[i]), w_in=w_perm,
            w_in_t=w_perm[:, :3 * QKV_W].T.reshape(3, QKV_W, D_MODEL),
            w_in_kuv=jnp.concatenate([w_perm[:, QKV_W:2 * QKV_W], w_perm[:, 3 * QKV_W:]], axis=1),
            mix_prompt=mix_prompt, mix_sample=mix_sample,
            w_out=w_out[i].astype(BF16), g_ffn=row2(norm_ffn[i]),
            w_router=_pad_router(w_router[j]) if moe else None, moe=moe,
            wg=(w_exp_gate[j] if moe else w_ffn_gate[j][None]).astype(BF16),
            wu=(w_exp_up[j] if moe else w_ffn_up[j][None]).astype(BF16),
            wd=(w_exp_down[j] if moe else w_ffn_down[j][None]).astype(BF16),
            g_ple=row2(norm_ple[i]), w_ple_gate=w_ple_gate[i].astype(BF16), w_ple_proj=w_ple_proj[i].astype(BF16),
            g_final=row2(norm_final), lam=lam.reshape(1, 1).astype(F32),
            g_diff=diff_norm[i].astype(F32), out_scale=1.0 - _lambda_init(i),
        ))

    bias_tiles = _prompt_bias_tiles(bias_table).reshape(KV_HEADS // 2, 2, N_BIAS_TILES, MOBA_BLOCK, MOBA_BLOCK)
    nbp = -(-nblk // 16) * 16

    def prompt_mix(i, lw, h):
        qt, kb, kt, vt, vtb, og, km = _inproj_prompt(h, lw["g_mix"], lw["w_in_t"], lw["w_in_kuv"], q_scale.T,
                                                     *lw["mix_prompt"], B=B, S=S, tm=TM_PROMPT)
        k3 = kb.reshape(B, S, QKV_W)
        kmean = jnp.pad(km.reshape(B, nblk, QKV_W), ((0, 0), (0, nbp - nblk), (0, 0)))
        om = _moba_prompt(qt, k3, vtb, kmean, bias_tiles)
        od = _diff_prompt(qt, k3, vtb, bias_tiles, lw["lam"], lw["g_diff"].reshape(DIFF_HEADS // 2, LANES, 1),
                          lw["out_scale"])
        return om.reshape(B * S, MOBA_W), od.reshape(B * S, DIFF_W), og, kt, vt, None

    y_p, k_p, v_p, _ = _trunk(x_prompt.reshape(B * S, D_MODEL), p_prompt, prompt_mix, W,
                              tm=TM_PROMPT, tm_f=TMF_PROMPT)

    cache_kt = jnp.transpose(cache_k, (0, 1, 3, 4, 2))
    cache_vt = jnp.transpose(cache_v, (0, 1, 3, 4, 2))
    sbias = _sample_bias(bias_table, past_len, n_tok)

    def sample_mix(i, lw, h):
        q, kf, vf, og, vg = _inproj_sample(h, lw["g_mix"], lw["w_in"], q_scale, *lw["mix_sample"], tm=TM_SAMPLE)
        pad_new = lambda a: jnp.pad(a.reshape(nseq, n_tok, QKV_W), ((0, 0), (0, LANES - n_tok), (0, 0)))
        o = _sample_attn(i, page_table, cache_kt, cache_vt, q.reshape(nseq, n_tok, QKV_W),
                         pad_new(kf), pad_new(vf), *sbias, lw["lam"], lw["g_diff"].reshape(1, DIFF_W),
                         lw["out_scale"])
        o = o.reshape(nseq * n_tok, QKV_W).astype(BF16)
        return o[:, :MOBA_W], o[:, MOBA_W:], og, kf, vf, vg

    y_s, k_s, v_s, g_s = _trunk(x_sample.reshape(nseq * n_tok, D_MODEL), p_sample, sample_mix, W,
                                tm=TM_SAMPLE, tm_f=TMF_SAMPLE)

    heads_t = lambda cols: jnp.transpose(
        jnp.stack(cols).reshape(depth, B, KV_HEADS, HEAD_DIM, S), (0, 1, 4, 2, 3))
    heads = lambda rows: jnp.stack(rows).reshape(depth, nseq, n_tok, KV_HEADS, HEAD_DIM)
    return (y_p.reshape(B, S, D_MODEL), y_s.reshape(nseq, n_tok, D_MODEL),
            heads_t(k_p), heads_t(v_p), heads(k_s), heads(v_s),
            jnp.stack(g_s).reshape(depth, nseq, n_tok, GMLP_W))
```

```python
import functools
import math

import jax
import jax.numpy as jnp
import numpy as np
from jax import lax
from jax.experimental import pallas as pl
from jax.experimental.pallas import tpu as pltpu

F32 = jnp.float32
BF16 = jnp.bfloat16

D_MODEL = 1024
HEAD_DIM = 64
MOBA_HEADS = 6
DIFF_HEADS = 4
KV_HEADS = MOBA_HEADS + DIFF_HEADS
MOBA_W = MOBA_HEADS * HEAD_DIM
DIFF_W = DIFF_HEADS * HEAD_DIM
GMLP_GROUPS = 6
GMLP_W = GMLP_GROUPS * HEAD_DIM
QKV_W = MOBA_W + DIFF_W
DIFF_QK = HEAD_DIM // 2
MOBA_BLOCK = 256
MOBA_TOPK = 3
GMLP_CHUNK = 128
NUM_BUCKETS = 32
MAX_EXACT = 16
MAX_DISTANCE = 2048
D_FF = 2816
N_EXPERTS = 8
TOP_K = 2
D_PLE = 256
PAGE_SIZE = 128
RMS_EPS = 1e-6

LANES = 128
NEG = -1e30
FF_CHUNK = D_FF // 2
N_BIAS_TILES = 8
VMEM_LIMIT = 56 * 1024 * 1024


def _cparams(sem):
    return pltpu.CompilerParams(dimension_semantics=sem, vmem_limit_bytes=VMEM_LIMIT)


def _rms(x, g):
    return x * lax.rsqrt(jnp.mean(x * x, axis=-1, keepdims=True) + RMS_EPS) * g


def _bucket_np(dist):
    n = np.maximum(dist, 0)
    nf = np.maximum(n, MAX_EXACT).astype(np.float32)
    large = MAX_EXACT + (np.log(nf / np.float32(MAX_EXACT)) / np.float32(math.log(MAX_DISTANCE / MAX_EXACT))
                         * np.float32(NUM_BUCKETS - MAX_EXACT)).astype(np.int32)
    return np.where(n < MAX_EXACT, n, np.minimum(large, NUM_BUCKETS - 1)).astype(np.int32)


def _bucket_thresholds():
    buckets = _bucket_np(np.arange(2 * MAX_DISTANCE))
    assert np.all(np.diff(buckets) >= 0) and buckets[-1] == NUM_BUCKETS - 1
    return [int(np.argmax(buckets >= b)) for b in range(NUM_BUCKETS)]


def _bias_of_distance(dist, value_of_bucket):
    thr = _bucket_thresholds()
    out = jnp.zeros(dist.shape, F32) + value_of_bucket(0)
    for b in range(1, NUM_BUCKETS):
        out = jnp.where(dist >= thr[b], value_of_bucket(b), out)
    return out


def _prompt_bias_kernel(tab_ref, o_ref):
    h, d = pl.program_id(0), pl.program_id(1)
    blk = MOBA_BLOCK
    kj = lax.broadcasted_iota(jnp.int32, (blk, blk), 0)
    qi = lax.broadcasted_iota(jnp.int32, (blk, blk), 1)
    dist = qi - kj + blk * d
    bias = _bias_of_distance(dist, lambda b: tab_ref[h, b])
    o_ref[0, 0] = jnp.where(dist >= 0, bias, NEG)


def _prompt_bias_tiles(bias_table):
    assert MOBA_BLOCK * (N_BIAS_TILES - 2) + 1 >= _bucket_thresholds()[-1]
    blk = MOBA_BLOCK
    return pl.pallas_call(
        _prompt_bias_kernel,
        grid=(KV_HEADS, N_BIAS_TILES),
        in_specs=[pl.BlockSpec(memory_space=pltpu.SMEM)],
        out_specs=pl.BlockSpec((1, 1, blk, blk), lambda h, d: (h, d, 0, 0)),
        out_shape=jax.ShapeDtypeStruct((KV_HEADS, N_BIAS_TILES, blk, blk), F32),
        compiler_params=_cparams(("parallel", "parallel")),
        name="prompt_bias",
    )(bias_table.astype(F32).T)


def _gmlp_tile(u, vg, ws_ref, bs_ref, og_ref, tm):
    vgb = vg.astype(BF16)
    first = lax.broadcasted_iota(jnp.int32, (1, LANES), 1) < HEAD_DIM
    for c in range(tm // GMLP_CHUNK):
        rows = slice(c * GMLP_CHUNK, (c + 1) * GMLP_CHUNK)
        for p in range(GMLP_GROUPS // 2):
            cols = slice(p * LANES, (p + 1) * LANES)
            vc = vgb[rows, cols]
            m0 = jnp.dot(ws_ref[2 * p], vc, preferred_element_type=F32)
            m1 = jnp.dot(ws_ref[2 * p + 1], vc, preferred_element_type=F32)
            mixed = jnp.where(first, m0, m1) + bs_ref[:, cols]
            og_ref[rows, cols] = (u[rows, cols] * mixed).astype(BF16)


def _inproj_prompt_kernel(x_ref, g_ref, wt_ref, w_ref, qs_ref, ws_ref, bs_ref,
                          qt_ref, kb_ref, kt_ref, vt_ref, vtb_ref, og_ref, km_ref, *, tm):
    xn = _rms(x_ref[...], g_ref[...]).astype(BF16)
    proj_t = lambda wt: lax.dot_general(wt, xn, (((1,), (1,)), ((), ())), preferred_element_type=F32)
    qt_ref[0] = (proj_t(wt_ref[0]) * qs_ref[...]).astype(BF16)
    kt_ref[0] = proj_t(wt_ref[1])
    vt = proj_t(wt_ref[2])
    vt_ref[0] = vt
    k = jnp.dot(xn, w_ref[:, :QKV_W], preferred_element_type=F32)
    kb_ref[...] = k.astype(BF16)
    for j in range(tm // MOBA_BLOCK):
        cols = slice(j * MOBA_BLOCK, (j + 1) * MOBA_BLOCK)
        vtb_ref[0, j] = vt[:, cols].astype(BF16)
        km_ref[j] = jnp.mean(k[cols], axis=0, keepdims=True)
    u = jax.nn.gelu(jnp.dot(xn, w_ref[:, QKV_W:QKV_W + GMLP_W], preferred_element_type=F32))
    vg = jax.nn.gelu(jnp.dot(xn, w_ref[:, QKV_W + GMLP_W:], preferred_element_type=F32))
    _gmlp_tile(u, vg, ws_ref, bs_ref, og_ref, tm)


def _inproj_prompt(x, g, w_t, w_kuv, q_scale_col, w_mix, b_mix, *, B, S, tm):
    assert S % tm == 0 and tm % MOBA_BLOCK == 0
    nj = S // tm
    nkm = tm // MOBA_BLOCK
    row = lambda w: pl.BlockSpec((tm, w), lambda b, j: (b * nj + j, 0))
    full = lambda a: pl.BlockSpec(a.shape, lambda b, j: (0,) * a.ndim)
    tcol = pl.BlockSpec((1, QKV_W, tm), lambda b, j: (b, 0, j))
    out_shape = (
        jax.ShapeDtypeStruct((B, QKV_W, S), BF16),
        jax.ShapeDtypeStruct((B * S, QKV_W), BF16),
        jax.ShapeDtypeStruct((B, QKV_W, S), F32),
        jax.ShapeDtypeStruct((B, QKV_W, S), F32),
        jax.ShapeDtypeStruct((B, S // MOBA_BLOCK, QKV_W, MOBA_BLOCK), BF16),
        jax.ShapeDtypeStruct((B * S, GMLP_W), BF16),
        jax.ShapeDtypeStruct((B * S // MOBA_BLOCK, 1, QKV_W), F32),
    )
    out_specs = (tcol, row(QKV_W), tcol, tcol,
                 pl.BlockSpec((1, nkm, QKV_W, MOBA_BLOCK), lambda b, j: (b, j, 0, 0)),
                 row(GMLP_W),
                 pl.BlockSpec((nkm, 1, QKV_W), lambda b, j: (b * nj + j, 0, 0)))
    return pl.pallas_call(
        functools.partial(_inproj_prompt_kernel, tm=tm),
        grid=(B, nj),
        in_specs=[row(D_MODEL), full(g), full(w_t), full(w_kuv), full(q_scale_col), full(w_mix), full(b_mix)],
        out_specs=out_specs,
        out_shape=out_shape,
        compiler_params=_cparams(("parallel", "parallel")),
        name="inproj_prompt",
    )(x, g, w_t, w_kuv, q_scale_col, w_mix, b_mix)


def _inproj_sample_kernel(x_ref, g_ref, w_ref, qs_ref, ws_ref, bs_ref, q_ref, kf_ref, vf_ref, og_ref, vg_ref, *, tm):
    xn = _rms(x_ref[...], g_ref[...]).astype(BF16)
    c1, c2, c3, c4 = QKV_W, 2 * QKV_W, 3 * QKV_W, 3 * QKV_W + GMLP_W
    q_ref[...] = jnp.dot(xn, w_ref[:, :c1], preferred_element_type=F32) * qs_ref[...]
    kf_ref[...] = jnp.dot(xn, w_ref[:, c1:c2], preferred_element_type=F32)
    vf_ref[...] = jnp.dot(xn, w_ref[:, c2:c3], preferred_element_type=F32)
    u = jax.nn.gelu(jnp.dot(xn, w_ref[:, c3:c4], preferred_element_type=F32))
    vg = jax.nn.gelu(jnp.dot(xn, w_ref[:, c4:], preferred_element_type=F32))
    vg_ref[...] = vg
    _gmlp_tile(u, vg, ws_ref, bs_ref, og_ref, tm)


def _inproj_sample(x, g, w_perm, q_scale, w_mix, b_mix, *, tm):
    T = x.shape[0]
    assert T % tm == 0 and tm % GMLP_CHUNK == 0
    row = lambda w: pl.BlockSpec((tm, w), lambda i: (i, 0))
    full = lambda a: pl.BlockSpec(a.shape, lambda i: (0,) * a.ndim)
    out_shape = (
        jax.ShapeDtypeStruct((T, QKV_W), F32),
        jax.ShapeDtypeStruct((T, QKV_W), F32),
        jax.ShapeDtypeStruct((T, QKV_W), F32),
        jax.ShapeDtypeStruct((T, GMLP_W), BF16),
        jax.ShapeDtypeStruct((T, GMLP_W), F32),
    )
    return pl.pallas_call(
        functools.partial(_inproj_sample_kernel, tm=tm),
        grid=(T // tm,),
        in_specs=[row(D_MODEL), full(g), full(w_perm), full(q_scale), full(w_mix), full(b_mix)],
        out_specs=(row(QKV_W), row(QKV_W), row(QKV_W), row(GMLP_W), row(GMLP_W)),
        out_shape=out_shape,
        compiler_params=_cparams(("parallel",)),
        name="inproj_sample",
    )(x, g, w_perm, q_scale, w_mix, b_mix)


def _masked_queries(qt, group_width, n_maps):
    sub = lax.broadcasted_iota(jnp.int32, (LANES, 1), 0)
    zero = jnp.zeros_like(qt)
    return jnp.concatenate([jnp.where(sub // group_width == j, qt, zero) for j in range(n_maps)], axis=1)


def _softmax_update(s, vt, m_ref, l_ref, acc_ref):
    blk = MOBA_BLOCK
    for c in range(s.shape[1] // blk):
        cols = slice(c * blk, (c + 1) * blk)
        sc = s[:, cols]
        m_old = m_ref[:, cols]
        m = jnp.maximum(m_old, jnp.max(sc, axis=0, keepdims=True))
        alpha = jnp.exp2(m_old - m)
        p = jnp.exp2(sc - m)
        m_ref[:, cols] = m
        l_ref[:, cols] = alpha * l_ref[:, cols] + jnp.sum(p, axis=0, keepdims=True)
        acc_ref[:, cols] = alpha * acc_ref[:, cols] + jnp.dot(vt, p.astype(BF16), preferred_element_type=F32)


def _attend_blocks(i, k_ref, vt_ref, qw_ref, s_refs, m_ref, l_ref, acc_ref, bias, mask):
    blk = MOBA_BLOCK
    s0_ref, s1_ref = s_refs

    def scores(n, dst):
        kb = k_ref[0, pl.ds(pl.multiple_of(jnp.minimum(n, i) * blk, blk), blk), :]
        dst[...] = jnp.dot(kb, qw_ref[...], preferred_element_type=F32)

    def update(n, src):
        s = src[...] + bias(jnp.minimum(i - n, N_BIAS_TILES - 1))
        _softmax_update(mask(s, n), vt_ref[0, n], m_ref, l_ref, acc_ref)

    m_ref[...] = jnp.full(m_ref.shape, NEG, F32)
    l_ref[...] = jnp.zeros(l_ref.shape, F32)
    acc_ref[...] = jnp.zeros(acc_ref.shape, F32)
    scores(0, s0_ref)

    def body(j, carry):
        n = 2 * j
        scores(n + 1, s1_ref)
        update(n, s0_ref)

        @pl.when(n + 1 <= i)
        def _():
            scores(n + 2, s0_ref)
            update(n + 1, s1_ref)

        return carry

    lax.fori_loop(0, (i + 2) // 2, body, 0)


def _qk(qm, k):
    return lax.dot_general(qm, k, (((1,), (1,)), ((), ())), preferred_element_type=F32)


def _split3(x):
    hi = x.astype(BF16)
    r = x - hi.astype(F32)
    mid = r.astype(BF16)
    lo = (r - mid.astype(F32)).astype(BF16)
    return hi, mid, lo


def _top3_mask(gate, pos, axis):
    sel = jnp.zeros(gate.shape, F32)
    for _ in range(MOBA_TOPK):
        m = jnp.max(gate, axis=axis, keepdims=True)
        idx = jnp.min(jnp.where(gate == m, pos, gate.shape[axis]), axis=axis, keepdims=True)
        hit = pos == idx
        sel = jnp.where(hit, 1.0, sel)
        gate = jnp.where(hit, -jnp.inf, gate)
    return sel


def _moba_prompt_kernel(qt_ref, k_ref, vt_ref, km_ref, bias_ref, o_ref,
                        qw_ref, s0_ref, s1_ref, m_ref, l_ref, acc_ref, sel_ref):
    i = pl.program_id(2)
    blk = MOBA_BLOCK
    dot = lambda a, b: jnp.dot(a, b, preferred_element_type=F32)
    qw_ref[...] = _masked_queries(qt_ref[0], HEAD_DIM, 2)
    qw = qw_ref[...]
    blk_id = lax.broadcasted_iota(jnp.int32, sel_ref.shape, 0)
    km_hi, km_mid, km_lo = _split3(km_ref[0])
    gate = dot(km_hi, qw) + dot(km_mid, qw) + dot(km_lo, qw)
    gate = jnp.where(blk_id < i, gate, -jnp.inf)
    sel_ref[...] = jnp.where(blk_id < i, _top3_mask(gate, blk_id, 0), jnp.where(blk_id == i, 1.0, 0.0))
    bias = lambda d: jnp.concatenate([bias_ref[0, 0, d], bias_ref[0, 1, d]], axis=1)
    mask = lambda s, n: jnp.where(sel_ref[pl.ds(n, 1), :] > 0.5, s, NEG)
    _attend_blocks(i, k_ref, vt_ref, qw_ref, (s0_ref, s1_ref), m_ref, l_ref, acc_ref, bias, mask)
    o = acc_ref[...] / l_ref[...]
    sub = lax.broadcasted_iota(jnp.int32, (LANES, 1), 0)
    o_ref[0] = jnp.where(sub < HEAD_DIM, o[:, :blk], o[:, blk:]).T.astype(BF16)


def _prompt_attn_specs(S, nbp, off):
    blk = MOBA_BLOCK
    return dict(
        qt=pl.BlockSpec((1, LANES, blk), lambda b, p, i: (b, p + off, i)),
        k=pl.BlockSpec((1, S, LANES), lambda b, p, i: (b, 0, p + off)),
        vt=pl.BlockSpec((1, S // blk, LANES, blk), lambda b, p, i: (b, 0, p + off, 0)),
        km=pl.BlockSpec((1, nbp, LANES), lambda b, p, i: (b, 0, p + off)),
        bias=pl.BlockSpec((1, 2, N_BIAS_TILES, blk, blk), lambda b, p, i: (p + off, 0, 0, 0, 0)),
        out=pl.BlockSpec((1, blk, LANES), lambda b, p, i: (b, i, p)),
    )


def _moba_prompt(qt, k, vtb, kmean, bias_tiles):
    B, S, _ = k.shape
    blk = MOBA_BLOCK
    nbp = kmean.shape[1]
    sp = _prompt_attn_specs(S, nbp, 0)
    return pl.pallas_call(
        _moba_prompt_kernel,
        grid=(B, MOBA_HEADS // 2, S // blk),
        in_specs=[sp["qt"], sp["k"], sp["vt"], sp["km"], sp["bias"]],
        out_specs=sp["out"],
        out_shape=jax.ShapeDtypeStruct((B, S, MOBA_W), BF16),
        scratch_shapes=[pltpu.VMEM((LANES, 2 * blk), BF16),
                        pltpu.VMEM((blk, 2 * blk), F32), pltpu.VMEM((blk, 2 * blk), F32),
                        pltpu.VMEM((1, 2 * blk), F32), pltpu.VMEM((1, 2 * blk), F32),
                        pltpu.VMEM((LANES, 2 * blk), F32), pltpu.VMEM((nbp, 2 * blk), F32)],
        compiler_params=_cparams(("parallel", "parallel", "arbitrary")),
        name="moba_prompt",
    )(qt, k, vtb, kmean, bias_tiles)


def _diff_prompt_kernel(lam_ref, qt_ref, k_ref, vt_ref, bias_ref, g_ref, o_ref,
                        qw_ref, s0_ref, s1_ref, m_ref, l_ref, acc_ref, *, out_scale):
    i = pl.program_id(2)
    blk = MOBA_BLOCK
    qw_ref[...] = _masked_queries(qt_ref[0], DIFF_QK, 4)
    bias = lambda d: jnp.concatenate([bias_ref[0, j // 2, d] for j in range(4)], axis=1)
    _attend_blocks(i, k_ref, vt_ref, qw_ref, (s0_ref, s1_ref), m_ref, l_ref, acc_ref, bias, lambda s, n: s)
    lam = lam_ref[0, 0]
    sub = lax.broadcasted_iota(jnp.int32, (LANES, 1), 0)
    o = acc_ref[...] / l_ref[...]
    outs = []
    for h in range(2):
        oh = o[:, (2 * h) * blk:(2 * h + 1) * blk] - lam * o[:, (2 * h + 1) * blk:(2 * h + 2) * blk]
        ms = jnp.sum(jnp.where(sub // HEAD_DIM == h, oh * oh, 0.0), axis=0, keepdims=True) * (1.0 / HEAD_DIM)
        outs.append(oh * lax.rsqrt(ms + RMS_EPS))
    ot = jnp.where(sub < HEAD_DIM, outs[0], outs[1]) * g_ref[0] * out_scale
    o_ref[0] = ot.T.astype(BF16)


def _diff_prompt(qt, k, vtb, bias_tiles, lam, g_col, out_scale):
    B, S, _ = k.shape
    blk = MOBA_BLOCK
    sp = _prompt_attn_specs(S, 0, MOBA_HEADS // 2)
    return pl.pallas_call(
        functools.partial(_diff_prompt_kernel, out_scale=out_scale),
        grid=(B, DIFF_HEADS // 2, S // blk),
        in_specs=[pl.BlockSpec(memory_space=pltpu.SMEM), sp["qt"], sp["k"], sp["vt"], sp["bias"],
                  pl.BlockSpec((1, LANES, 1), lambda b, p, i: (p, 0, 0))],
        out_specs=sp["out"],
        out_shape=jax.ShapeDtypeStruct((B, S, DIFF_W), BF16),
        scratch_shapes=[pltpu.VMEM((LANES, 4 * blk), BF16),
                        pltpu.VMEM((blk, 4 * blk), F32), pltpu.VMEM((blk, 4 * blk), F32),
                        pltpu.VMEM((1, 4 * blk), F32), pltpu.VMEM((1, 4 * blk), F32),
                        pltpu.VMEM((LANES, 4 * blk), F32)],
        compiler_params=_cparams(("parallel", "parallel", "arbitrary")),
        name="diff_prompt",
    )(lam, qt, k, vtb, bias_tiles, g_col)


def _post_a_kernel(x_ref, om_ref, od_ref, og_ref, wo_ref, g_ref, *rest):
    h = x_ref[...]
    h = h + jnp.dot(om_ref[...], wo_ref[0:MOBA_W, :], preferred_element_type=F32)
    h = h + jnp.dot(od_ref[...], wo_ref[MOBA_W:QKV_W, :], preferred_element_type=F32)
    h = h + jnp.dot(og_ref[...], wo_ref[QKV_W:, :], preferred_element_type=F32)
    if len(rest) == 2:
        h_ref, hn_ref = rest
    else:
        wr_ref, h_ref, hn_ref, route_ref = rest
    h_ref[...] = h
    hn = _rms(h, g_ref[...])
    hn_ref[...] = hn.astype(BF16)
    if len(rest) == 2:
        return
    a_hi, a_mid, _ = _split3(hn)
    dot = lambda a, b: jnp.dot(a, b, preferred_element_type=F32)
    logits = dot(a_hi, wr_ref[0]) + (dot(a_hi, wr_ref[1]) + dot(a_mid, wr_ref[0]))
    col = lax.broadcasted_iota(jnp.int32, logits.shape, 1)
    logits = jnp.where(col < N_EXPERTS, logits, -jnp.inf)
    m1 = jnp.max(logits, axis=1, keepdims=True)
    i1 = jnp.min(jnp.where(logits == m1, col, LANES), axis=1, keepdims=True)
    rest = jnp.where(col == i1, -jnp.inf, logits)
    m2 = jnp.max(rest, axis=1, keepdims=True)
    i2 = jnp.min(jnp.where(rest == m2, col, LANES), axis=1, keepdims=True)
    e = jnp.exp(m2 - m1)
    w1 = 1.0 / (1.0 + e)
    w2 = e / (1.0 + e)
    route = jnp.where(col == 0, i1.astype(F32), 0.0)
    route = jnp.where(col == 1, i2.astype(F32), route)
    route = jnp.where(col == 2, w1, route)
    route = jnp.where(col == 3, w2, route)
    route_ref[...] = route


def _post_a(x, om, od, og, w_out, g, w_router3, *, tm):
    T = x.shape[0]
    row = lambda w: pl.BlockSpec((tm, w), lambda i: (i, 0))
    full = lambda a: pl.BlockSpec(a.shape, lambda i: (0,) * a.ndim)
    routed = w_router3 is not None
    outs = pl.pallas_call(
        _post_a_kernel,
        grid=(T // tm,),
        in_specs=[row(D_MODEL), row(MOBA_W), row(DIFF_W), row(GMLP_W), full(w_out), full(g)]
                 + ([full(w_router3)] if routed else []),
        out_specs=(row(D_MODEL), row(D_MODEL)) + ((row(LANES),) if routed else ()),
        out_shape=(jax.ShapeDtypeStruct((T, D_MODEL), F32), jax.ShapeDtypeStruct((T, D_MODEL), BF16))
                  + ((jax.ShapeDtypeStruct((T, LANES), F32),) if routed else ()),
        compiler_params=_cparams(("parallel",)),
        name="post_a",
    )(x, om, od, og, w_out, g, *([w_router3] if routed else []))
    return outs if routed else (*outs, None)


def _ffn_kernel(te_ref, nu_ref, x_ref, wg_ref, wu_ref, wd_ref, sc_ref, o_ref, acc_ref):
    t = pl.program_id(0)
    f = pl.program_id(1)

    @pl.when(t < nu_ref[0])
    def _():
        x = x_ref[...]
        g = jnp.dot(x, wg_ref[0], preferred_element_type=F32)
        u = jnp.dot(x, wu_ref[0], preferred_element_type=F32)
        a = (g * jax.nn.sigmoid(g) * u).astype(BF16)
        part = jnp.dot(a, wd_ref[0], preferred_element_type=F32)

        @pl.when(f == 0)
        def _():
            acc_ref[...] = part

        @pl.when(f == 1)
        def _():
            o_ref[...] = (acc_ref[...] + part) * sc_ref[...]

    @pl.when(jnp.logical_and(t >= nu_ref[0], f == 1))
    def _():
        o_ref[...] = jnp.zeros_like(o_ref)


def _ffn(tile_expert, n_used, x, wg, wu, wd, scale, *, tm):
    N = x.shape[0]
    n_tiles = N // tm
    chunk = lambda t, f: jnp.where(t % 2 == 0, f, 1 - f)
    return pl.pallas_call(
        _ffn_kernel,
        grid_spec=pltpu.PrefetchScalarGridSpec(
            num_scalar_prefetch=2,
            grid=(n_tiles, 2),
            in_specs=[
                pl.BlockSpec((tm, D_MODEL), lambda t, f, te, nu: (t, 0)),
                pl.BlockSpec((1, D_MODEL, FF_CHUNK), lambda t, f, te, nu: (te[t], 0, chunk(t, f))),
                pl.BlockSpec((1, D_MODEL, FF_CHUNK), lambda t, f, te, nu: (te[t], 0, chunk(t, f))),
                pl.BlockSpec((1, FF_CHUNK, D_MODEL), lambda t, f, te, nu: (te[t], chunk(t, f), 0)),
                pl.BlockSpec((tm, 1), lambda t, f, te, nu: (t, 0)),
            ],
            out_specs=pl.BlockSpec((tm, D_MODEL), lambda t, f, te, nu: (t, 0)),
            scratch_shapes=[pltpu.VMEM((tm, D_MODEL), F32)],
        ),
        out_shape=jax.ShapeDtypeStruct((N, D_MODEL), F32),
        compiler_params=_cparams(("arbitrary", "arbitrary")),
        name="ffn",
    )(tile_expert, n_used, x, wg, wu, wd, scale)


def _moe_dispatch(route, *, tm):
    T = route.shape[0]
    n_assign = T * TOP_K
    n_tiles = -(-n_assign // tm) + N_EXPERTS
    experts = route[:, :TOP_K].astype(jnp.int32).reshape(n_assign)
    weights = route[:, TOP_K:2 * TOP_K].reshape(n_assign)
    onehot = (experts[:, None] == jnp.arange(N_EXPERTS)[None, :]).astype(jnp.int32)
    rank = jnp.sum((jnp.cumsum(onehot, axis=0) - onehot) * onehot, axis=1)
    counts = jnp.sum(onehot, axis=0)
    tiles_per = (counts + tm - 1) // tm
    tile_end = jnp.cumsum(tiles_per)
    slot = ((tile_end - tiles_per) * tm)[experts] + rank
    tile_expert = jnp.minimum(jnp.searchsorted(tile_end, jnp.arange(n_tiles), side="right"),
                              N_EXPERTS - 1).astype(jnp.int32)
    n_used = tile_end[-1:].astype(jnp.int32)
    token = jnp.arange(n_assign, dtype=jnp.int32) // TOP_K
    slot_token = jnp.zeros((n_tiles * tm,), jnp.int32).at[slot].set(token)
    slot_scale = jnp.zeros((n_tiles * tm,), F32).at[slot].set(weights)
    return tile_expert, n_used, slot_token, slot_scale[:, None], slot.reshape(T, TOP_K)


def _post_b_kernel(*refs, n_y, final):
    h_ref = refs[0]
    y_refs = refs[1:1 + n_y]
    p_ref, gp_ref, wgate_ref, wproj_ref, gf_ref, o_ref = refs[1 + n_y:]
    h = h_ref[...]
    for y_ref in y_refs:
        h = h + y_ref[...]
    gate = jax.nn.sigmoid(jnp.dot(_rms(h, gp_ref[...]).astype(BF16), wgate_ref[...], preferred_element_type=F32))
    h = h + gate * jnp.dot(p_ref[...].astype(BF16), wproj_ref[...], preferred_element_type=F32)
    o_ref[...] = _rms(h, gf_ref[...]) if final else h


def _post_b(h, ys, p, g_ple, w_gate, w_proj, g_final, *, tm, final):
    T = h.shape[0]
    row = lambda w: pl.BlockSpec((tm, w), lambda i: (i, 0))
    full = lambda a: pl.BlockSpec(a.shape, lambda i: (0,) * a.ndim)
    return pl.pallas_call(
        functools.partial(_post_b_kernel, n_y=len(ys), final=final),
        grid=(T // tm,),
        in_specs=[row(D_MODEL)] + [row(D_MODEL)] * len(ys)
                 + [row(D_PLE), full(g_ple), full(w_gate), full(w_proj), full(g_final)],
        out_specs=row(D_MODEL),
        out_shape=jax.ShapeDtypeStruct((T, D_MODEL), F32),
        compiler_params=_cparams(("parallel",)),
        name="post_b",
    )(h, *ys, p, g_ple, w_gate, w_proj, g_final)


PAGES_PER_STEP = 8
DIFF_MAPS = 2 * DIFF_HEADS


def _sample_attn_kernel(pt_ref, lam_ref, *refs, n_tok, n_steps, out_scale):
    P = PAGES_PER_STEP
    k_refs = refs[:P]
    v_refs = refs[P:2 * P]
    (q_ref, kn_ref, vn_ref, bm_ref, bd_ref, bmo_ref, bdo_ref, g_ref, o_ref,
     qm_ref, qd_ref, gate_ref, mb_ref, lb_ref, accb_ref, m_ref, l_ref, acc_ref) = refs[2 * P:]
    s = pl.program_id(1)
    rm = MOBA_HEADS * n_tok
    rd = DIFF_MAPS * n_tok
    lane = lax.broadcasted_iota(jnp.int32, (1, LANES), 1)

    @pl.when(s == 0)
    def _():
        q = q_ref[0]
        qm = jnp.concatenate([q[:, :MOBA_W]] * MOBA_HEADS, axis=0)
        r = lax.broadcasted_iota(jnp.int32, qm.shape, 0) // n_tok
        c = lax.broadcasted_iota(jnp.int32, qm.shape, 1) // HEAD_DIM
        qm_ref[...] = jnp.where(r == c, qm, 0.0).astype(BF16)
        qd = jnp.concatenate([q[:, MOBA_W:]] * DIFF_MAPS, axis=0)
        r = lax.broadcasted_iota(jnp.int32, qd.shape, 0) // n_tok
        c = lax.broadcasted_iota(jnp.int32, qd.shape, 1) // DIFF_QK
        qd_ref[...] = jnp.where(r == c, qd, 0.0).astype(BF16)
        gate_ref[...] = jnp.full(gate_ref.shape, -jnp.inf, F32)
        mb_ref[...] = jnp.full(mb_ref.shape, NEG, F32)
        lb_ref[...] = jnp.zeros(lb_ref.shape, F32)
        m_ref[...] = jnp.full(m_ref.shape, NEG, F32)
        l_ref[...] = jnp.zeros(l_ref.shape, F32)
        acc_ref[...] = jnp.zeros(acc_ref.shape, F32)

    def scores(q, k, transposed):
        return jnp.dot(q, k, preferred_element_type=F32) if transposed else _qk(q, k)

    def weighted(p, v, transposed):
        return _qk(p, v) if transposed else jnp.dot(p, v, preferred_element_type=F32)

    def diff_update(kd, vd, bias, transposed):
        sd = scores(qd_ref[...], kd, transposed) + bias
        m_old = m_ref[...]
        m = jnp.maximum(m_old, jnp.max(sd, axis=1, keepdims=True))
        alpha = jnp.exp(m_old - m)
        p = jnp.exp(sd - m)
        m_ref[...] = m
        l_ref[...] = alpha * l_ref[...] + jnp.sum(p, axis=1, keepdims=True)
        acc_ref[...] = alpha * acc_ref[...] + weighted(p.astype(BF16), vd, transposed)

    def moba_partial(km, vm, bias, transposed):
        raw = scores(qm_ref[...], km, transposed)
        sm = raw + bias
        m = jnp.max(sm, axis=1, keepdims=True)
        p = jnp.exp(sm - m)
        return raw, m, jnp.sum(p, axis=1, keepdims=True), weighted(p.astype(BF16), vm, transposed)

    page = lambda ref: ref[0, 0].reshape(QKV_W, PAGE_SIZE)
    kt = jnp.concatenate([page(r) for r in k_refs], axis=1).astype(BF16)
    vt = jnp.concatenate([page(r) for r in v_refs], axis=1).astype(BF16)
    blk = MOBA_BLOCK
    raw = jnp.dot(qm_ref[...], kt[:MOBA_W], preferred_element_type=F32)
    gate, mb, lb = gate_ref[...], mb_ref[...], lb_ref[...]
    for bi in range(P // 2):
        n = s * (P // 2) + bi
        cols = slice(bi * blk, (bi + 1) * blk)
        sm = raw[:, cols] + bm_ref[n]
        m = jnp.max(sm, axis=1, keepdims=True)
        p = jnp.exp(sm - m)
        accb_ref[n] = _qk(p.astype(BF16), vt[:MOBA_W, cols])
        gate = jnp.where(lane == n, jnp.mean(raw[:, cols], axis=1, keepdims=True), gate)
        mb = jnp.where(lane == n, m, mb)
        lb = jnp.where(lane == n, jnp.sum(p, axis=1, keepdims=True), lb)
    gate_ref[...], mb_ref[...], lb_ref[...] = gate, mb, lb
    bias_d = jnp.concatenate([bd_ref[s * (P // 2) + bi] for bi in range(P // 2)], axis=1)
    diff_update(kt[MOBA_W:], vt[MOBA_W:], bias_d, True)

    @pl.when(s == n_steps - 1)
    def _():
        kn = kn_ref[0].astype(BF16)
        vn = vn_ref[0].astype(BF16)
        diff_update(kn[:, MOBA_W:], vn[:, MOBA_W:], bdo_ref[...], False)
        _, m_own, l_own, acc_own = moba_partial(kn[:, :MOBA_W], vn[:, :MOBA_W], bmo_ref[...], False)
        col = lax.broadcasted_iota(jnp.int32, (rm, LANES), 1)
        sel = _top3_mask(gate_ref[...], col, 1)
        sel = jnp.where(gate_ref[...] > -jnp.inf, sel, 0.0)
        mb = mb_ref[...]
        m_fin = jnp.maximum(m_own, jnp.max(jnp.where(sel > 0.5, mb, NEG), axis=1, keepdims=True))
        w = sel * jnp.exp(jnp.minimum(mb - m_fin, 0.0))
        w_own = jnp.exp(m_own - m_fin)
        l_fin = w_own * l_own + jnp.sum(w * lb_ref[...], axis=1, keepdims=True)

        def merge(nb, acc):
            wn = jnp.sum(jnp.where(col == nb, w, 0.0), axis=1, keepdims=True)
            return acc + wn * accb_ref[nb]

        acc_fin = lax.fori_loop(0, n_steps * (P // 2), merge, w_own * acc_own)
        om_rows = acc_fin / l_fin
        lane_m = lax.broadcasted_iota(jnp.int32, (n_tok, MOBA_W), 1) // HEAD_DIM
        om = jnp.zeros((n_tok, MOBA_W), F32)
        for h in range(MOBA_HEADS):
            om = jnp.where(lane_m == h, om_rows[h * n_tok:(h + 1) * n_tok], om)

        od_rows = acc_ref[...] / l_ref[...]
        lane_d = lax.broadcasted_iota(jnp.int32, (n_tok, DIFF_W), 1) // HEAD_DIM
        lam = lam_ref[0, 0]
        od = jnp.zeros((n_tok, DIFF_W), F32)
        for h in range(DIFF_HEADS):
            a1 = od_rows[(2 * h) * n_tok:(2 * h + 1) * n_tok]
            a2 = od_rows[(2 * h + 1) * n_tok:(2 * h + 2) * n_tok]
            od = jnp.where(lane_d == h, a1 - lam * a2, od)
        normed = jnp.zeros_like(od)
        for h in range(DIFF_HEADS):
            ms = jnp.sum(jnp.where(lane_d == h, od * od, 0.0), axis=1, keepdims=True) * (1.0 / HEAD_DIM)
            normed = jnp.where(lane_d == h, od * lax.rsqrt(ms + RMS_EPS), normed)
        o_ref[0, :, :MOBA_W] = om
        o_ref[0, :, MOBA_W:] = normed * g_ref[...] * out_scale


def _sample_attn(layer, page_table, cache_kt, cache_vt, q, k_new, v_new,
                 bias_m, bias_d, bias_mo, bias_do, lam, g, out_scale):
    nseq, n_pages = page_table.shape
    n_tok = q.shape[1]
    P = PAGES_PER_STEP
    assert n_pages % P == 0 and n_pages // 2 <= LANES and n_tok % 8 == 0
    n_steps = n_pages // P
    nb = n_pages // 2
    rm, rd = MOBA_HEADS * n_tok, DIFF_MAPS * n_tok
    page = lambda j: pl.BlockSpec((1, 1, KV_HEADS, HEAD_DIM, PAGE_SIZE),
                                  lambda b, s, pt, j=j: (layer, pt[b, s * P + j], 0, 0, 0))
    const = lambda a: pl.BlockSpec(a.shape, lambda b, s, pt: (0,) * a.ndim)
    per_seq = lambda a: pl.BlockSpec((1,) + a.shape[1:], lambda b, s, pt: (b,) + (0,) * (a.ndim - 1))
    return pl.pallas_call(
        functools.partial(_sample_attn_kernel, n_tok=n_tok, n_steps=n_steps, out_scale=out_scale),
        grid_spec=pltpu.PrefetchScalarGridSpec(
            num_scalar_prefetch=1,
            grid=(nseq, n_steps),
            in_specs=[pl.BlockSpec(memory_space=pltpu.SMEM)]
                     + [page(j) for j in range(P)] + [page(j) for j in range(P)]
                     + [per_seq(q), per_seq(k_new), per_seq(v_new),
                        const(bias_m), const(bias_d), const(bias_mo), const(bias_do), const(g)],
            out_specs=pl.BlockSpec((1, n_tok, QKV_W), lambda b, s, pt: (b, 0, 0)),
            scratch_shapes=[
                pltpu.VMEM((rm, MOBA_W), BF16), pltpu.VMEM((rd, DIFF_W), BF16),
                pltpu.VMEM((rm, LANES), F32), pltpu.VMEM((rm, LANES), F32), pltpu.VMEM((rm, LANES), F32),
                pltpu.VMEM((nb, rm, MOBA_W), F32),
                pltpu.VMEM((rd, 1), F32), pltpu.VMEM((rd, 1), F32), pltpu.VMEM((rd, DIFF_W), F32),
            ],
        ),
        out_shape=jax.ShapeDtypeStruct((nseq, n_tok, QKV_W), F32),
        compiler_params=_cparams(("parallel", "arbitrary")),
        name="sample_attn",
    )(page_table, lam, *([cache_kt] * P), *([cache_vt] * P), q, k_new, v_new,
      bias_m, bias_d, bias_mo, bias_do, g)


def _sample_bias_kernel(rowtab_ref, o_ref, *, past_len, n_tok):
    n = pl.program_id(0)
    rows, blk = o_ref.shape[1], o_ref.shape[2]
    t = lax.broadcasted_iota(jnp.int32, (rows, blk), 0) % n_tok
    pos = n * blk + lax.broadcasted_iota(jnp.int32, (rows, blk), 1)
    o_ref[0] = _bias_of_distance(past_len + t - pos, lambda b: rowtab_ref[:, b:b + 1])


def _sample_bias(bias_table, past_len, n_tok):
    tab = bias_table.astype(F32).T
    tab_m = jnp.repeat(tab[:MOBA_HEADS], n_tok, axis=0)
    tab_d = jnp.repeat(tab[MOBA_HEADS:], 2 * n_tok, axis=0)

    def past(rowtab):
        rows = rowtab.shape[0]
        return pl.pallas_call(
            functools.partial(_sample_bias_kernel, past_len=past_len, n_tok=n_tok),
            grid=(past_len // MOBA_BLOCK,),
            in_specs=[pl.BlockSpec(rowtab.shape, lambda n: (0, 0))],
            out_specs=pl.BlockSpec((1, rows, MOBA_BLOCK), lambda n: (n, 0, 0)),
            out_shape=jax.ShapeDtypeStruct((past_len // MOBA_BLOCK, rows, MOBA_BLOCK), F32),
            compiler_params=_cparams(("parallel",)),
            name="sample_bias",
        )(rowtab)

    t = np.arange(n_tok)
    d_own = t[:, None] - np.arange(LANES)[None, :]
    valid = (d_own >= 0) & (np.arange(LANES)[None, :] < n_tok)

    def own(rowtab):
        reps = rowtab.shape[0] // n_tok
        picked = jnp.take_along_axis(rowtab, jnp.asarray(np.tile(_bucket_np(d_own), (reps, 1))), axis=1)
        return jnp.where(jnp.asarray(np.tile(valid, (reps, 1))), picked, NEG)

    return past(tab_m), past(tab_d), own(tab_m), own(tab_d)


def _lambda_init(i):
    return 0.8 - 0.6 * math.exp(-0.3 * i)


def _perm_w_in(w):
    sizes = (MOBA_W, MOBA_W, MOBA_W, DIFF_W, DIFF_W, DIFF_W, GMLP_W, GMLP_W)
    o = np.concatenate([[0], np.cumsum(sizes)])
    qm, km, vm, qd, kd, vd, u, vg = [w[:, o[j]:o[j + 1]] for j in range(8)]
    return jnp.concatenate([qm, qd, km, kd, vm, vd, u, vg], axis=1).astype(BF16)


def _pad_router(w):
    wp = jnp.pad(w.astype(F32), ((0, 0), (0, LANES - N_EXPERTS)))
    return jnp.stack(_split3(wp))


TM_PROMPT, TM_SAMPLE = 512, 256
TMF_PROMPT, TMF_SAMPLE = 512, 128


def _trunk(h, p_layers, mix_fn, W, *, tm, tm_f):
    T = h.shape[0]
    depth = len(W)
    ks, vs, gs = [], [], []
    for i, lw in enumerate(W):
        om, od, og, k_out, v_out, vg = mix_fn(i, lw, h)
        h1, hn, route = _post_a(h, om, od, og, lw["w_out"], lw["g_ffn"], lw["w_router"], tm=tm)
        if lw["moe"]:
            te, nu, slot_token, slot_scale, slot = _moe_dispatch(route, tm=tm_f)
            y = _ffn(te, nu, jnp.take(hn, slot_token, axis=0), lw["wg"], lw["wu"], lw["wd"], slot_scale, tm=tm_f)
            ys = [jnp.take(y, slot[:, j], axis=0) for j in range(TOP_K)]
        else:
            n_tiles = T // tm_f
            ys = [_ffn(jnp.zeros((n_tiles,), jnp.int32), jnp.full((1,), n_tiles, jnp.int32), hn,
                       lw["wg"], lw["wu"], lw["wd"], jnp.ones((T, 1), F32), tm=tm_f)]
        h = _post_b(h1, ys, p_layers[i].reshape(T, D_PLE), lw["g_ple"], lw["w_ple_gate"], lw["w_ple_proj"],
                    lw["g_final"], tm=tm, final=(i == depth - 1))
        ks.append(k_out)
        vs.append(v_out)
        gs.append(vg)
    return h, ks, vs, gs


def kernel(x_prompt, x_sample, cache_k, cache_v, page_table, p_prompt, p_sample, bias_table, norm_mix, w_in, w_out, lambda_q1, lambda_k1, lambda_q2, lambda_k2, diff_norm, gmlp_w_s, gmlp_b_s, norm_ffn, w_ffn_gate, w_ffn_up, w_ffn_down, w_router, w_exp_gate, w_exp_up, w_exp_down, norm_ple, w_ple_gate, w_ple_proj, norm_final):
    B, S, _ = x_prompt.shape
    nseq, n_tok, _ = x_sample.shape
    depth = cache_k.shape[0]
    n_pages = page_table.shape[1]
    past_len = n_pages * PAGE_SIZE
    nblk = S // MOBA_BLOCK
    assert S % TM_PROMPT == 0 and nblk <= LANES and past_len % MOBA_BLOCK == 0
    assert GMLP_CHUNK % n_tok == 0 and (nseq * n_tok) % TM_SAMPLE == 0

    row2 = lambda a: a.astype(F32).reshape(1, -1)
    q_scale = jnp.asarray(np.concatenate([np.full(MOBA_W, HEAD_DIM ** -0.5, np.float32),
                                          np.full(DIFF_W, DIFF_QK ** -0.5, np.float32)])[None])
    tril = jnp.asarray(np.tril(np.ones((GMLP_CHUNK, GMLP_CHUNK), np.float32)))
    reps = GMLP_CHUNK // n_tok
    W = []
    for i in range(depth):
        ws = gmlp_w_s[i].astype(F32) * tril
        bs = gmlp_b_s[i].astype(F32)
        mix_prompt = (ws.astype(BF16), jnp.repeat(bs.T, HEAD_DIM, axis=1))
        w_small = ws[:, :n_tok, :n_tok]
        w_blockdiag = jnp.einsum("ab,gts->gatbs", jnp.eye(reps, dtype=F32), w_small).reshape(
            GMLP_GROUPS, GMLP_CHUNK, GMLP_CHUNK)
        mix_sample = (w_blockdiag.astype(BF16), jnp.tile(jnp.repeat(bs[:, :n_tok].T, HEAD_DIM, axis=1), (reps, 1)))
        lam = (jnp.exp(jnp.sum(lambda_q1[i].astype(F32) * lambda_k1[i].astype(F32)))
               - jnp.exp(jnp.sum(lambda_q2[i].astype(F32) * lambda_k2[i].astype(F32))) + _lambda_init(i))
        moe = i % 2 == 1
        j = i // 2
        w_perm = _perm_w_in(w_in[i])
        W.append(dict(
            g_mix=row2(norm_mix[i]), w_in=w_perm,
            w_in_t=w_perm[:, :3 * QKV_W].T.reshape(3, QKV_W, D_MODEL),
            w_in_kuv=jnp.concatenate([w_perm[:, QKV_W:2 * QKV_W], w_perm[:, 3 * QKV_W:]], axis=1),
            mix_prompt=mix_prompt, mix_sample=mix_sample,
            w_out=w_out[i].astype(BF16), g_ffn=row2(norm_ffn[i]),
            w_router=_pad_router(w_router[j]) if moe else None, moe=moe,
            wg=(w_exp_gate[j] if moe else w_ffn_gate[j][None]).astype(BF16),
            wu=(w_exp_up[j] if moe else w_ffn_up[j][None]).astype(BF16),
            wd=(w_exp_down[j] if moe else w_ffn_down[j][None]).astype(BF16),
            g_ple=row2(norm_ple[i]), w_ple_gate=w_ple_gate[i].astype(BF16), w_ple_proj=w_ple_proj[i].astype(BF16),
            g_final=row2(norm_final), lam=lam.reshape(1, 1).astype(F32),
            g_diff=diff_norm[i].astype(F32), out_scale=1.0 - _lambda_init(i),
        ))

    log2e = math.log2(math.e)
    bias_tiles = _prompt_bias_tiles(bias_table.astype(F32) * log2e).reshape(
        KV_HEADS // 2, 2, N_BIAS_TILES, MOBA_BLOCK, MOBA_BLOCK)
    q_scale_prompt = (q_scale * log2e).T
    nbp = -(-nblk // 16) * 16

    def prompt_mix(i, lw, h):
        qt, kb, kt, vt, vtb, og, km = _inproj_prompt(h, lw["g_mix"], lw["w_in_t"], lw["w_in_kuv"], q_scale_prompt,
                                                     *lw["mix_prompt"], B=B, S=S, tm=TM_PROMPT)
        k3 = kb.reshape(B, S, QKV_W)
        kmean = jnp.pad(km.reshape(B, nblk, QKV_W), ((0, 0), (0, nbp - nblk), (0, 0)))
        om = _moba_prompt(qt, k3, vtb, kmean, bias_tiles)
        od = _diff_prompt(qt, k3, vtb, bias_tiles, lw["lam"], lw["g_diff"].reshape(DIFF_HEADS // 2, LANES, 1),
                          lw["out_scale"])
        return om.reshape(B * S, MOBA_W), od.reshape(B * S, DIFF_W), og, kt, vt, None

    y_p, k_p, v_p, _ = _trunk(x_prompt.reshape(B * S, D_MODEL), p_prompt, prompt_mix, W,
                              tm=TM_PROMPT, tm_f=TMF_PROMPT)

    cache_kt = jnp.transpose(cache_k, (0, 1, 3, 4, 2))
    cache_vt = jnp.transpose(cache_v, (0, 1, 3, 4, 2))
    sbias = _sample_bias(bias_table, past_len, n_tok)

    def sample_mix(i, lw, h):
        q, kf, vf, og, vg = _inproj_sample(h, lw["g_mix"], lw["w_in"], q_scale, *lw["mix_sample"], tm=TM_SAMPLE)
        pad_new = lambda a: jnp.pad(a.reshape(nseq, n_tok, QKV_W), ((0, 0), (0, LANES - n_tok), (0, 0)))
        o = _sample_attn(i, page_table, cache_kt, cache_vt, q.reshape(nseq, n_tok, QKV_W),
                         pad_new(kf), pad_new(vf), *sbias, lw["lam"], lw["g_diff"].reshape(1, DIFF_W),
                         lw["out_scale"])
        o = o.reshape(nseq * n_tok, QKV_W).astype(BF16)
        return o[:, :MOBA_W], o[:, MOBA_W:], og, kf, vf, vg

    y_s, k_s, v_s, g_s = _trunk(x_sample.reshape(nseq * n_tok, D_MODEL), p_sample, sample_mix, W,
                                tm=TM_SAMPLE, tm_f=TMF_SAMPLE)

    heads_t = lambda cols: jnp.transpose(
        jnp.stack(cols).reshape(depth, B, KV_HEADS, HEAD_DIM, S), (0, 1, 4, 2, 3))
    heads = lambda rows: jnp.stack(rows).reshape(depth, nseq, n_tok, KV_HEADS, HEAD_DIM)
    return (y_p.reshape(B, S, D_MODEL), y_s.reshape(nseq, n_tok, D_MODEL),
            heads_t(k_p), heads_t(v_p), heads(k_s), heads(v_s),
            jnp.stack(g_s).reshape(depth, nseq, n_tok, GMLP_W))
```

```python
import functools
import math

import jax
import jax.numpy as jnp
import numpy as np
from jax import lax
from jax.experimental import pallas as pl
from jax.experimental.pallas import tpu as pltpu

F32 = jnp.float32
BF16 = jnp.bfloat16

D_MODEL = 1024
HEAD_DIM = 64
MOBA_HEADS = 6
DIFF_HEADS = 4
KV_HEADS = MOBA_HEADS + DIFF_HEADS
MOBA_W = MOBA_HEADS * HEAD_DIM
DIFF_W = DIFF_HEADS * HEAD_DIM
GMLP_GROUPS = 6
GMLP_W = GMLP_GROUPS * HEAD_DIM
QKV_W = MOBA_W + DIFF_W
DIFF_QK = HEAD_DIM // 2
MOBA_BLOCK = 256
MOBA_TOPK = 3
GMLP_CHUNK = 128
NUM_BUCKETS = 32
MAX_EXACT = 16
MAX_DISTANCE = 2048
D_FF = 2816
N_EXPERTS = 8
TOP_K = 2
D_PLE = 256
PAGE_SIZE = 128
RMS_EPS = 1e-6

LANES = 128
NEG = -1e30
FF_CHUNK = D_FF // 2
N_BIAS_TILES = 8
VMEM_LIMIT = 56 * 1024 * 1024


def _cparams(sem):
    return pltpu.CompilerParams(dimension_semantics=sem, vmem_limit_bytes=VMEM_LIMIT)


def _rms(x, g):
    return x * lax.rsqrt(jnp.mean(x * x, axis=-1, keepdims=True) + RMS_EPS) * g


def _bucket_np(dist):
    n = np.maximum(dist, 0)
    nf = np.maximum(n, MAX_EXACT).astype(np.float32)
    large = MAX_EXACT + (np.log(nf / np.float32(MAX_EXACT)) / np.float32(math.log(MAX_DISTANCE / MAX_EXACT))
                         * np.float32(NUM_BUCKETS - MAX_EXACT)).astype(np.int32)
    return np.where(n < MAX_EXACT, n, np.minimum(large, NUM_BUCKETS - 1)).astype(np.int32)


def _bucket_thresholds():
    buckets = _bucket_np(np.arange(2 * MAX_DISTANCE))
    assert np.all(np.diff(buckets) >= 0) and buckets[-1] == NUM_BUCKETS - 1
    return [int(np.argmax(buckets >= b)) for b in range(NUM_BUCKETS)]


def _bias_of_distance(dist, value_of_bucket):
    thr = _bucket_thresholds()
    out = jnp.zeros(dist.shape, F32) + value_of_bucket(0)
    for b in range(1, NUM_BUCKETS):
        out = jnp.where(dist >= thr[b], value_of_bucket(b), out)
    return out


def _prompt_bias_kernel(tab_ref, o_ref):
    h, d = pl.program_id(0), pl.program_id(1)
    blk = MOBA_BLOCK
    kj = lax.broadcasted_iota(jnp.int32, (blk, blk), 0)
    qi = lax.broadcasted_iota(jnp.int32, (blk, blk), 1)
    dist = qi - kj + blk * d
    bias = _bias_of_distance(dist, lambda b: tab_ref[h, b])
    o_ref[0, 0] = jnp.where(dist >= 0, bias, NEG)


def _prompt_bias_tiles(bias_table):
    assert MOBA_BLOCK * (N_BIAS_TILES - 2) + 1 >= _bucket_thresholds()[-1]
    blk = MOBA_BLOCK
    return pl.pallas_call(
        _prompt_bias_kernel,
        grid=(KV_HEADS, N_BIAS_TILES),
        in_specs=[pl.BlockSpec(memory_space=pltpu.SMEM)],
        out_specs=pl.BlockSpec((1, 1, blk, blk), lambda h, d: (h, d, 0, 0)),
        out_shape=jax.ShapeDtypeStruct((KV_HEADS, N_BIAS_TILES, blk, blk), F32),
        compiler_params=_cparams(("parallel", "parallel")),
        name="prompt_bias",
    )(bias_table.astype(F32).T)


def _gmlp_tile(u, vg, ws_ref, bs_ref, og_ref, tm):
    vgb = vg.astype(BF16)
    first = lax.broadcasted_iota(jnp.int32, (1, LANES), 1) < HEAD_DIM
    for c in range(tm // GMLP_CHUNK):
        rows = slice(c * GMLP_CHUNK, (c + 1) * GMLP_CHUNK)
        for p in range(GMLP_GROUPS // 2):
            cols = slice(p * LANES, (p + 1) * LANES)
            vc = vgb[rows, cols]
            m0 = jnp.dot(ws_ref[2 * p], vc, preferred_element_type=F32)
            m1 = jnp.dot(ws_ref[2 * p + 1], vc, preferred_element_type=F32)
            mixed = jnp.where(first, m0, m1) + bs_ref[:, cols]
            og_ref[rows, cols] = (u[rows, cols] * mixed).astype(BF16)


def _inproj_prompt_kernel(x_ref, g_ref, wt_ref, w_ref, qs_ref, ws_ref, bs_ref,
                          qt_ref, kb_ref, kt_ref, vt_ref, vtb_ref, og_ref, km_ref, *, tm):
    xn = _rms(x_ref[...], g_ref[...]).astype(BF16)
    proj_t = lambda wt: lax.dot_general(wt, xn, (((1,), (1,)), ((), ())), preferred_element_type=F32)
    qt_ref[0] = (proj_t(wt_ref[0]) * qs_ref[...]).astype(BF16)
    kt_ref[0] = proj_t(wt_ref[1])
    vt = proj_t(wt_ref[2])
    vt_ref[0] = vt
    k = jnp.dot(xn, w_ref[:, :QKV_W], preferred_element_type=F32)
    kb_ref[...] = k.astype(BF16)
    for j in range(tm // MOBA_BLOCK):
        cols = slice(j * MOBA_BLOCK, (j + 1) * MOBA_BLOCK)
        vtb_ref[0, j] = vt[:, cols].astype(BF16)
        km_ref[j] = jnp.mean(k[cols], axis=0, keepdims=True)
    u = jax.nn.gelu(jnp.dot(xn, w_ref[:, QKV_W:QKV_W + GMLP_W], preferred_element_type=F32))
    vg = jax.nn.gelu(jnp.dot(xn, w_ref[:, QKV_W + GMLP_W:], preferred_element_type=F32))
    _gmlp_tile(u, vg, ws_ref, bs_ref, og_ref, tm)


def _inproj_prompt(x, g, w_t, w_kuv, q_scale_col, w_mix, b_mix, *, B, S, tm):
    assert S % tm == 0 and tm % MOBA_BLOCK == 0
    nj = S // tm
    nkm = tm // MOBA_BLOCK
    row = lambda w: pl.BlockSpec((tm, w), lambda b, j: (b * nj + j, 0))
    full = lambda a: pl.BlockSpec(a.shape, lambda b, j: (0,) * a.ndim)
    tcol = pl.BlockSpec((1, QKV_W, tm), lambda b, j: (b, 0, j))
    out_shape = (
        jax.ShapeDtypeStruct((B, QKV_W, S), BF16),
        jax.ShapeDtypeStruct((B * S, QKV_W), BF16),
        jax.ShapeDtypeStruct((B, QKV_W, S), F32),
        jax.ShapeDtypeStruct((B, QKV_W, S), F32),
        jax.ShapeDtypeStruct((B, S // MOBA_BLOCK, QKV_W, MOBA_BLOCK), BF16),
        jax.ShapeDtypeStruct((B * S, GMLP_W), BF16),
        jax.ShapeDtypeStruct((B * S // MOBA_BLOCK, 1, QKV_W), F32),
    )
    out_specs = (tcol, row(QKV_W), tcol, tcol,
                 pl.BlockSpec((1, nkm, QKV_W, MOBA_BLOCK), lambda b, j: (b, j, 0, 0)),
                 row(GMLP_W),
                 pl.BlockSpec((nkm, 1, QKV_W), lambda b, j: (b * nj + j, 0, 0)))
    return pl.pallas_call(
        functools.partial(_inproj_prompt_kernel, tm=tm),
        grid=(B, nj),
        in_specs=[row(D_MODEL), full(g), full(w_t), full(w_kuv), full(q_scale_col), full(w_mix), full(b_mix)],
        out_specs=out_specs,
        out_shape=out_shape,
        compiler_params=_cparams(("parallel", "parallel")),
        name="inproj_prompt",
    )(x, g, w_t, w_kuv, q_scale_col, w_mix, b_mix)


def _inproj_sample_kernel(x_ref, g_ref, w_ref, qs_ref, ws_ref, bs_ref, q_ref, kf_ref, vf_ref, og_ref, vg_ref, *, tm):
    xn = _rms(x_ref[...], g_ref[...]).astype(BF16)
    c1, c2, c3, c4 = QKV_W, 2 * QKV_W, 3 * QKV_W, 3 * QKV_W + GMLP_W
    q_ref[...] = jnp.dot(xn, w_ref[:, :c1], preferred_element_type=F32) * qs_ref[...]
    kf_ref[...] = jnp.dot(xn, w_ref[:, c1:c2], preferred_element_type=F32)
    vf_ref[...] = jnp.dot(xn, w_ref[:, c2:c3], preferred_element_type=F32)
    u = jax.nn.gelu(jnp.dot(xn, w_ref[:, c3:c4], preferred_element_type=F32))
    vg = jax.nn.gelu(jnp.dot(xn, w_ref[:, c4:], preferred_element_type=F32))
    vg_ref[...] = vg
    _gmlp_tile(u, vg, ws_ref, bs_ref, og_ref, tm)


def _inproj_sample(x, g, w_perm, q_scale, w_mix, b_mix, *, tm):
    T = x.shape[0]
    assert T % tm == 0 and tm % GMLP_CHUNK == 0
    row = lambda w: pl.BlockSpec((tm, w), lambda i: (i, 0))
    full = lambda a: pl.BlockSpec(a.shape, lambda i: (0,) * a.ndim)
    out_shape = (
        jax.ShapeDtypeStruct((T, QKV_W), F32),
        jax.ShapeDtypeStruct((T, QKV_W), F32),
        jax.ShapeDtypeStruct((T, QKV_W), F32),
        jax.ShapeDtypeStruct((T, GMLP_W), BF16),
        jax.ShapeDtypeStruct((T, GMLP_W), F32),
    )
    return pl.pallas_call(
        functools.partial(_inproj_sample_kernel, tm=tm),
        grid=(T // tm,),
        in_specs=[row(D_MODEL), full(g), full(w_perm), full(q_scale), full(w_mix), full(b_mix)],
        out_specs=(row(QKV_W), row(QKV_W), row(QKV_W), row(GMLP_W), row(GMLP_W)),
        out_shape=out_shape,
        compiler_params=_cparams(("parallel",)),
        name="inproj_sample",
    )(x, g, w_perm, q_scale, w_mix, b_mix)


def _masked_queries(qt, group_width, n_maps):
    sub = lax.broadcasted_iota(jnp.int32, (LANES, 1), 0)
    zero = jnp.zeros_like(qt)
    return jnp.concatenate([jnp.where(sub // group_width == j, qt, zero) for j in range(n_maps)], axis=1)


def _softmax_update(s, vt, m_ref, l_ref, acc_ref):
    blk = MOBA_BLOCK
    n_maps = s.shape[1] // blk
    for c in range(n_maps):
        cols = slice(c * blk, (c + 1) * blk)
        head = c * 2 // n_maps
        sc = s[:, cols]
        m_old = m_ref[:, cols]
        m = jnp.maximum(m_old, jnp.max(sc, axis=0, keepdims=True))
        alpha = jnp.exp2(m_old - m)
        p = jnp.exp2(sc - m)
        m_ref[:, cols] = m
        l_ref[:, cols] = alpha * l_ref[:, cols] + jnp.sum(p, axis=0, keepdims=True)
        pv = jnp.dot(vt[head * HEAD_DIM:(head + 1) * HEAD_DIM], p.astype(BF16), preferred_element_type=F32)
        acc_ref[:, cols] = alpha * acc_ref[:, cols] + pv


def _attend_blocks(i, k_ref, vt_ref, qw_ref, s_refs, m_ref, l_ref, acc_ref, bias, mask):
    blk = MOBA_BLOCK
    s0_ref, s1_ref = s_refs

    def scores(n, dst):
        kb = k_ref[0, pl.ds(pl.multiple_of(jnp.minimum(n, i) * blk, blk), blk), :]
        dst[...] = jnp.dot(kb, qw_ref[...], preferred_element_type=F32)

    def update(n, src):
        s = src[...] + bias(jnp.minimum(i - n, N_BIAS_TILES - 1))
        _softmax_update(mask(s, n), vt_ref[0, n], m_ref, l_ref, acc_ref)

    m_ref[...] = jnp.full(m_ref.shape, NEG, F32)
    l_ref[...] = jnp.zeros(l_ref.shape, F32)
    acc_ref[...] = jnp.zeros(acc_ref.shape, F32)
    scores(0, s0_ref)

    def body(j, carry):
        n = 2 * j
        scores(n + 1, s1_ref)
        update(n, s0_ref)

        @pl.when(n + 1 <= i)
        def _():
            scores(n + 2, s0_ref)
            update(n + 1, s1_ref)

        return carry

    lax.fori_loop(0, (i + 2) // 2, body, 0)


def _qk(qm, k):
    return lax.dot_general(qm, k, (((1,), (1,)), ((), ())), preferred_element_type=F32)


def _split3(x):
    hi = x.astype(BF16)
    r = x - hi.astype(F32)
    mid = r.astype(BF16)
    lo = (r - mid.astype(F32)).astype(BF16)
    return hi, mid, lo


def _top3_mask(gate, pos, axis):
    sel = jnp.zeros(gate.shape, F32)
    for _ in range(MOBA_TOPK):
        m = jnp.max(gate, axis=axis, keepdims=True)
        idx = jnp.min(jnp.where(gate == m, pos, gate.shape[axis]), axis=axis, keepdims=True)
        hit = pos == idx
        sel = jnp.where(hit, 1.0, sel)
        gate = jnp.where(hit, -jnp.inf, gate)
    return sel


def _moba_prompt_kernel(qt_ref, k_ref, vt_ref, km_ref, bias_ref, o_ref,
                        qw_ref, s0_ref, s1_ref, m_ref, l_ref, acc_ref, sel_ref):
    i = pl.program_id(2)
    blk = MOBA_BLOCK
    dot = lambda a, b: jnp.dot(a, b, preferred_element_type=F32)
    qw_ref[...] = _masked_queries(qt_ref[0], HEAD_DIM, 2)
    qw = qw_ref[...]
    blk_id = lax.broadcasted_iota(jnp.int32, sel_ref.shape, 0)
    km_hi, km_mid, km_lo = _split3(km_ref[0])
    gate = dot(km_hi, qw) + dot(km_mid, qw) + dot(km_lo, qw)
    gate = jnp.where(blk_id < i, gate, -jnp.inf)
    sel_ref[...] = jnp.where(blk_id < i, _top3_mask(gate, blk_id, 0), jnp.where(blk_id == i, 1.0, 0.0))
    bias = lambda d: jnp.concatenate([bias_ref[0, 0, d], bias_ref[0, 1, d]], axis=1)
    mask = lambda s, n: jnp.where(sel_ref[pl.ds(n, 1), :] > 0.5, s, NEG)
    _attend_blocks(i, k_ref, vt_ref, qw_ref, (s0_ref, s1_ref), m_ref, l_ref, acc_ref, bias, mask)
    o = acc_ref[...] / l_ref[...]
    o_ref[0] = jnp.concatenate([o[:, :blk], o[:, blk:]], axis=0).T.astype(BF16)


def _prompt_attn_specs(S, nbp, off):
    blk = MOBA_BLOCK
    return dict(
        qt=pl.BlockSpec((1, LANES, blk), lambda b, p, i: (b, p + off, i)),
        k=pl.BlockSpec((1, S, LANES), lambda b, p, i: (b, 0, p + off)),
        vt=pl.BlockSpec((1, S // blk, LANES, blk), lambda b, p, i: (b, 0, p + off, 0)),
        km=pl.BlockSpec((1, nbp, LANES), lambda b, p, i: (b, 0, p + off)),
        bias=pl.BlockSpec((1, 2, N_BIAS_TILES, blk, blk), lambda b, p, i: (p + off, 0, 0, 0, 0)),
        out=pl.BlockSpec((1, blk, LANES), lambda b, p, i: (b, i, p)),
    )


def _moba_prompt(qt, k, vtb, kmean, bias_tiles):
    B, S, _ = k.shape
    blk = MOBA_BLOCK
    nbp = kmean.shape[1]
    sp = _prompt_attn_specs(S, nbp, 0)
    return pl.pallas_call(
        _moba_prompt_kernel,
        grid=(B, MOBA_HEADS // 2, S // blk),
        in_specs=[sp["qt"], sp["k"], sp["vt"], sp["km"], sp["bias"]],
        out_specs=sp["out"],
        out_shape=jax.ShapeDtypeStruct((B, S, MOBA_W), BF16),
        scratch_shapes=[pltpu.VMEM((LANES, 2 * blk), BF16),
                        pltpu.VMEM((blk, 2 * blk), F32), pltpu.VMEM((blk, 2 * blk), F32),
                        pltpu.VMEM((1, 2 * blk), F32), pltpu.VMEM((1, 2 * blk), F32),
                        pltpu.VMEM((HEAD_DIM, 2 * blk), F32), pltpu.VMEM((nbp, 2 * blk), F32)],
        compiler_params=_cparams(("parallel", "parallel", "arbitrary")),
        name="moba_prompt",
    )(qt, k, vtb, kmean, bias_tiles)


def _diff_prompt_kernel(lam_ref, qt_ref, k_ref, vt_ref, bias_ref, g_ref, o_ref,
                        qw_ref, s0_ref, s1_ref, m_ref, l_ref, acc_ref, *, out_scale):
    i = pl.program_id(2)
    blk = MOBA_BLOCK
    qw_ref[...] = _masked_queries(qt_ref[0], DIFF_QK, 4)
    bias = lambda d: jnp.concatenate([bias_ref[0, j // 2, d] for j in range(4)], axis=1)
    _attend_blocks(i, k_ref, vt_ref, qw_ref, (s0_ref, s1_ref), m_ref, l_ref, acc_ref, bias, lambda s, n: s)
    lam = lam_ref[0, 0]
    o = acc_ref[...] / l_ref[...]
    outs = []
    for h in range(2):
        oh = o[:, (2 * h) * blk:(2 * h + 1) * blk] - lam * o[:, (2 * h + 1) * blk:(2 * h + 2) * blk]
        outs.append(oh * lax.rsqrt(jnp.mean(oh * oh, axis=0, keepdims=True) + RMS_EPS))
    ot = jnp.concatenate(outs, axis=0) * g_ref[0] * out_scale
    o_ref[0] = ot.T.astype(BF16)


def _diff_prompt(qt, k, vtb, bias_tiles, lam, g_col, out_scale):
    B, S, _ = k.shape
    blk = MOBA_BLOCK
    sp = _prompt_attn_specs(S, 0, MOBA_HEADS // 2)
    return pl.pallas_call(
        functools.partial(_diff_prompt_kernel, out_scale=out_scale),
        grid=(B, DIFF_HEADS // 2, S // blk),
        in_specs=[pl.BlockSpec(memory_space=pltpu.SMEM), sp["qt"], sp["k"], sp["vt"], sp["bias"],
                  pl.BlockSpec((1, LANES, 1), lambda b, p, i: (p, 0, 0))],
        out_specs=sp["out"],
        out_shape=jax.ShapeDtypeStruct((B, S, DIFF_W), BF16),
        scratch_shapes=[pltpu.VMEM((LANES, 4 * blk), BF16),
                        pltpu.VMEM((blk, 4 * blk), F32), pltpu.VMEM((blk, 4 * blk), F32),
                        pltpu.VMEM((1, 4 * blk), F32), pltpu.VMEM((1, 4 * blk), F32),
                        pltpu.VMEM((HEAD_DIM, 4 * blk), F32)],
        compiler_params=_cparams(("parallel", "parallel", "arbitrary")),
        name="diff_prompt",
    )(lam, qt, k, vtb, bias_tiles, g_col)


def _post_a_kernel(x_ref, om_ref, od_ref, og_ref, wo_ref, g_ref, *rest):
    h = x_ref[...]
    h = h + jnp.dot(om_ref[...], wo_ref[0:MOBA_W, :], preferred_element_type=F32)
    h = h + jnp.dot(od_ref[...], wo_ref[MOBA_W:QKV_W, :], preferred_element_type=F32)
    h = h + jnp.dot(og_ref[...], wo_ref[QKV_W:, :], preferred_element_type=F32)
    if len(rest) == 2:
        h_ref, hn_ref = rest
    else:
        wr_ref, h_ref, hn_ref, route_ref = rest
    h_ref[...] = h
    hn = _rms(h, g_ref[...])
    hn_ref[...] = hn.astype(BF16)
    if len(rest) == 2:
        return
    a_hi, a_mid, _ = _split3(hn)
    dot = lambda a, b: jnp.dot(a, b, preferred_element_type=F32)
    logits = dot(a_hi, wr_ref[0]) + (dot(a_hi, wr_ref[1]) + dot(a_mid, wr_ref[0]))
    col = lax.broadcasted_iota(jnp.int32, logits.shape, 1)
    logits = jnp.where(col < N_EXPERTS, logits, -jnp.inf)
    m1 = jnp.max(logits, axis=1, keepdims=True)
    i1 = jnp.min(jnp.where(logits == m1, col, LANES), axis=1, keepdims=True)
    rest = jnp.where(col == i1, -jnp.inf, logits)
    m2 = jnp.max(rest, axis=1, keepdims=True)
    i2 = jnp.min(jnp.where(rest == m2, col, LANES), axis=1, keepdims=True)
    e = jnp.exp(m2 - m1)
    w1 = 1.0 / (1.0 + e)
    w2 = e / (1.0 + e)
    route = jnp.where(col == 0, i1.astype(F32), 0.0)
    route = jnp.where(col == 1, i2.astype(F32), route)
    route = jnp.where(col == 2, w1, route)
    route = jnp.where(col == 3, w2, route)
    route_ref[...] = route


def _post_a(x, om, od, og, w_out, g, w_router3, *, tm):
    T = x.shape[0]
    row = lambda w: pl.BlockSpec((tm, w), lambda i: (i, 0))
    full = lambda a: pl.BlockSpec(a.shape, lambda i: (0,) * a.ndim)
    routed = w_router3 is not None
    outs = pl.pallas_call(
        _post_a_kernel,
        grid=(T // tm,),
        in_specs=[row(D_MODEL), row(MOBA_W), row(DIFF_W), row(GMLP_W), full(w_out), full(g)]
                 + ([full(w_router3)] if routed else []),
        out_specs=(row(D_MODEL), row(D_MODEL)) + ((row(LANES),) if routed else ()),
        out_shape=(jax.ShapeDtypeStruct((T, D_MODEL), F32), jax.ShapeDtypeStruct((T, D_MODEL), BF16))
                  + ((jax.ShapeDtypeStruct((T, LANES), F32),) if routed else ()),
        compiler_params=_cparams(("parallel",)),
        name="post_a",
    )(x, om, od, og, w_out, g, *([w_router3] if routed else []))
    return outs if routed else (*outs, None)


def _ffn_kernel(te_ref, nu_ref, x_ref, wg_ref, wu_ref, wd_ref, sc_ref, o_ref, acc_ref):
    t = pl.program_id(0)
    f = pl.program_id(1)

    @pl.when(t < nu_ref[0])
    def _():
        x = x_ref[...]
        g = jnp.dot(x, wg_ref[0], preferred_element_type=F32)
        u = jnp.dot(x, wu_ref[0], preferred_element_type=F32)
        a = (g * jax.nn.sigmoid(g) * u).astype(BF16)
        part = jnp.dot(a, wd_ref[0], preferred_element_type=F32)

        @pl.when(f == 0)
        def _():
            acc_ref[...] = part

        @pl.when(f == 1)
        def _():
            o_ref[...] = (acc_ref[...] + part) * sc_ref[...]

    @pl.when(jnp.logical_and(t >= nu_ref[0], f == 1))
    def _():
        o_ref[...] = jnp.zeros_like(o_ref)


def _ffn(tile_expert, n_used, x, wg, wu, wd, scale, *, tm):
    N = x.shape[0]
    n_tiles = N // tm
    chunk = lambda t, f: jnp.where(t % 2 == 0, f, 1 - f)
    return pl.pallas_call(
        _ffn_kernel,
        grid_spec=pltpu.PrefetchScalarGridSpec(
            num_scalar_prefetch=2,
            grid=(n_tiles, 2),
            in_specs=[
                pl.BlockSpec((tm, D_MODEL), lambda t, f, te, nu: (t, 0)),
                pl.BlockSpec((1, D_MODEL, FF_CHUNK), lambda t, f, te, nu: (te[t], 0, chunk(t, f))),
                pl.BlockSpec((1, D_MODEL, FF_CHUNK), lambda t, f, te, nu: (te[t], 0, chunk(t, f))),
                pl.BlockSpec((1, FF_CHUNK, D_MODEL), lambda t, f, te, nu: (te[t], chunk(t, f), 0)),
                pl.BlockSpec((tm, 1), lambda t, f, te, nu: (t, 0)),
            ],
            out_specs=pl.BlockSpec((tm, D_MODEL), lambda t, f, te, nu: (t, 0)),
            scratch_shapes=[pltpu.VMEM((tm, D_MODEL), F32)],
        ),
        out_shape=jax.ShapeDtypeStruct((N, D_MODEL), F32),
        compiler_params=_cparams(("arbitrary", "arbitrary")),
        name="ffn",
    )(tile_expert, n_used, x, wg, wu, wd, scale)


def _moe_dispatch(route, *, tm):
    T = route.shape[0]
    n_assign = T * TOP_K
    n_tiles = -(-n_assign // tm) + N_EXPERTS
    experts = route[:, :TOP_K].astype(jnp.int32).reshape(n_assign)
    weights = route[:, TOP_K:2 * TOP_K].reshape(n_assign)
    onehot = (experts[:, None] == jnp.arange(N_EXPERTS)[None, :]).astype(jnp.int32)
    rank = jnp.sum((jnp.cumsum(onehot, axis=0) - onehot) * onehot, axis=1)
    counts = jnp.sum(onehot, axis=0)
    tiles_per = (counts + tm - 1) // tm
    tile_end = jnp.cumsum(tiles_per)
    slot = ((tile_end - tiles_per) * tm)[experts] + rank
    tile_expert = jnp.minimum(jnp.searchsorted(tile_end, jnp.arange(n_tiles), side="right"),
                              N_EXPERTS - 1).astype(jnp.int32)
    n_used = tile_end[-1:].astype(jnp.int32)
    token = jnp.arange(n_assign, dtype=jnp.int32) // TOP_K
    slot_token = jnp.zeros((n_tiles * tm,), jnp.int32).at[slot].set(token)
    slot_scale = jnp.zeros((n_tiles * tm,), F32).at[slot].set(weights)
    return tile_expert, n_used, slot_token, slot_scale[:, None], slot.reshape(T, TOP_K)


def _post_b_kernel(*refs, n_y, final):
    h_ref = refs[0]
    y_refs = refs[1:1 + n_y]
    p_ref, gp_ref, wgate_ref, wproj_ref, gf_ref, o_ref = refs[1 + n_y:]
    h = h_ref[...]
    for y_ref in y_refs:
        h = h + y_ref[...]
    gate = jax.nn.sigmoid(jnp.dot(_rms(h, gp_ref[...]).astype(BF16), wgate_ref[...], preferred_element_type=F32))
    h = h + gate * jnp.dot(p_ref[...].astype(BF16), wproj_ref[...], preferred_element_type=F32)
    o_ref[...] = _rms(h, gf_ref[...]) if final else h


def _post_b(h, ys, p, g_ple, w_gate, w_proj, g_final, *, tm, final):
    T = h.shape[0]
    row = lambda w: pl.BlockSpec((tm, w), lambda i: (i, 0))
    full = lambda a: pl.BlockSpec(a.shape, lambda i: (0,) * a.ndim)
    return pl.pallas_call(
        functools.partial(_post_b_kernel, n_y=len(ys), final=final),
        grid=(T // tm,),
        in_specs=[row(D_MODEL)] + [row(D_MODEL)] * len(ys)
                 + [row(D_PLE), full(g_ple), full(w_gate), full(w_proj), full(g_final)],
        out_specs=row(D_MODEL),
        out_shape=jax.ShapeDtypeStruct((T, D_MODEL), F32),
        compiler_params=_cparams(("parallel",)),
        name="post_b",
    )(h, *ys, p, g_ple, w_gate, w_proj, g_final)


PAGES_PER_STEP = 16
BIAS_TAIL_BLOCKS = 8
DIFF_MAPS = 2 * DIFF_HEADS


def _sample_attn_kernel(pt_ref, lam_ref, *refs, n_tok, n_steps, n_tail, P, out_scale):
    k_refs = refs[:P]
    v_refs = refs[P:2 * P]
    (q_ref, kn_ref, vn_ref, bm_ref, bd_ref, bmf_ref, bdf_ref, bmo_ref, bdo_ref, g_ref, o_ref,
     qm_ref, qd_ref, gate_ref, mb_ref, lb_ref, accb_ref, m_ref, l_ref, acc_ref) = refs[2 * P:]
    s = pl.program_id(1)
    rm = MOBA_HEADS * n_tok
    rd = DIFF_MAPS * n_tok
    lane = lax.broadcasted_iota(jnp.int32, (1, LANES), 1)

    @pl.when(s == 0)
    def _():
        q = q_ref[0]
        qm = jnp.concatenate([q[:, :MOBA_W]] * MOBA_HEADS, axis=0)
        r = lax.broadcasted_iota(jnp.int32, qm.shape, 0) // n_tok
        c = lax.broadcasted_iota(jnp.int32, qm.shape, 1) // HEAD_DIM
        qm_ref[...] = jnp.where(r == c, qm, 0.0).astype(BF16)
        qd = jnp.concatenate([q[:, MOBA_W:]] * DIFF_MAPS, axis=0)
        r = lax.broadcasted_iota(jnp.int32, qd.shape, 0) // n_tok
        c = lax.broadcasted_iota(jnp.int32, qd.shape, 1) // DIFF_QK
        qd_ref[...] = jnp.where(r == c, qd, 0.0).astype(BF16)
        gate_ref[...] = jnp.full(gate_ref.shape, -jnp.inf, F32)
        mb_ref[...] = jnp.full(mb_ref.shape, NEG, F32)
        lb_ref[...] = jnp.zeros(lb_ref.shape, F32)
        m_ref[...] = jnp.full(m_ref.shape, NEG, F32)
        l_ref[...] = jnp.zeros(l_ref.shape, F32)
        acc_ref[...] = jnp.zeros(acc_ref.shape, F32)

    def scores(q, k, transposed):
        return jnp.dot(q, k, preferred_element_type=F32) if transposed else _qk(q, k)

    def weighted(p, v, transposed):
        return _qk(p, v) if transposed else jnp.dot(p, v, preferred_element_type=F32)

    def diff_update(kd, vd, bias, transposed):
        sd = scores(qd_ref[...], kd, transposed) + bias
        m_old = m_ref[...]
        m = jnp.maximum(m_old, jnp.max(sd, axis=1, keepdims=True))
        alpha = jnp.exp(m_old - m)
        p = jnp.exp(sd - m)
        m_ref[...] = m
        l_ref[...] = alpha * l_ref[...] + jnp.sum(p, axis=1, keepdims=True)
        acc_ref[...] = alpha * acc_ref[...] + weighted(p.astype(BF16), vd, transposed)

    def moba_partial(km, vm, bias, transposed):
        raw = scores(qm_ref[...], km, transposed)
        sm = raw + bias
        m = jnp.max(sm, axis=1, keepdims=True)
        p = jnp.exp(sm - m)
        return raw, m, jnp.sum(p, axis=1, keepdims=True), weighted(p.astype(BF16), vm, transposed)

    page = lambda ref: ref[0, 0].reshape(QKV_W, PAGE_SIZE)
    kt = jnp.concatenate([page(r) for r in k_refs], axis=1).astype(BF16)
    vt = jnp.concatenate([page(r) for r in v_refs], axis=1).astype(BF16)
    blk = MOBA_BLOCK
    nb = n_steps * (P // 2)

    def bias_of(n, tail_ref, far_ref):
        t = n - (nb - n_tail)
        return jnp.where(t >= 0, tail_ref[jnp.maximum(t, 0)], far_ref[...])

    raw = jnp.dot(qm_ref[...], kt[:MOBA_W], preferred_element_type=F32)
    gate, mb, lb = gate_ref[...], mb_ref[...], lb_ref[...]
    for bi in range(P // 2):
        n = s * (P // 2) + bi
        cols = slice(bi * blk, (bi + 1) * blk)
        sm = raw[:, cols] + bias_of(n, bm_ref, bmf_ref)
        m = jnp.max(sm, axis=1, keepdims=True)
        p = jnp.exp(sm - m)
        accb_ref[n] = _qk(p.astype(BF16), vt[:MOBA_W, cols])
        gate = jnp.where(lane == n, jnp.mean(raw[:, cols], axis=1, keepdims=True), gate)
        mb = jnp.where(lane == n, m, mb)
        lb = jnp.where(lane == n, jnp.sum(p, axis=1, keepdims=True), lb)
    gate_ref[...], mb_ref[...], lb_ref[...] = gate, mb, lb
    bias_d = jnp.concatenate([bias_of(s * (P // 2) + bi, bd_ref, bdf_ref) for bi in range(P // 2)], axis=1)
    diff_update(kt[MOBA_W:], vt[MOBA_W:], bias_d, True)

    @pl.when(s == n_steps - 1)
    def _():
        kn = kn_ref[0].astype(BF16)
        vn = vn_ref[0].astype(BF16)
        diff_update(kn[:, MOBA_W:], vn[:, MOBA_W:], bdo_ref[...], False)
        _, m_own, l_own, acc_own = moba_partial(kn[:, :MOBA_W], vn[:, :MOBA_W], bmo_ref[...], False)
        col = lax.broadcasted_iota(jnp.int32, (rm, LANES), 1)
        sel = _top3_mask(gate_ref[...], col, 1)
        sel = jnp.where(gate_ref[...] > -jnp.inf, sel, 0.0)
        mb = mb_ref[...]
        m_fin = jnp.maximum(m_own, jnp.max(jnp.where(sel > 0.5, mb, NEG), axis=1, keepdims=True))
        w = sel * jnp.exp(jnp.minimum(mb - m_fin, 0.0))
        w_own = jnp.exp(m_own - m_fin)
        l_fin = w_own * l_own + jnp.sum(w * lb_ref[...], axis=1, keepdims=True)

        def merge(nb, acc):
            wn = jnp.sum(jnp.where(col == nb, w, 0.0), axis=1, keepdims=True)
            return acc + wn * accb_ref[nb]

        acc_fin = lax.fori_loop(0, n_steps * (P // 2), merge, w_own * acc_own)
        om_rows = acc_fin / l_fin
        lane_m = lax.broadcasted_iota(jnp.int32, (n_tok, MOBA_W), 1) // HEAD_DIM
        om = jnp.zeros((n_tok, MOBA_W), F32)
        for h in range(MOBA_HEADS):
            om = jnp.where(lane_m == h, om_rows[h * n_tok:(h + 1) * n_tok], om)

        od_rows = acc_ref[...] / l_ref[...]
        lane_d = lax.broadcasted_iota(jnp.int32, (n_tok, DIFF_W), 1) // HEAD_DIM
        lam = lam_ref[0, 0]
        od = jnp.zeros((n_tok, DIFF_W), F32)
        for h in range(DIFF_HEADS):
            a1 = od_rows[(2 * h) * n_tok:(2 * h + 1) * n_tok]
            a2 = od_rows[(2 * h + 1) * n_tok:(2 * h + 2) * n_tok]
            od = jnp.where(lane_d == h, a1 - lam * a2, od)
        normed = jnp.zeros_like(od)
        for h in range(DIFF_HEADS):
            ms = jnp.sum(jnp.where(lane_d == h, od * od, 0.0), axis=1, keepdims=True) * (1.0 / HEAD_DIM)
            normed = jnp.where(lane_d == h, od * lax.rsqrt(ms + RMS_EPS), normed)
        o_ref[0, :, :MOBA_W] = om
        o_ref[0, :, MOBA_W:] = normed * g_ref[...] * out_scale


def _sample_attn(layer, page_table, cache_kt, cache_vt, q, k_new, v_new,
                 bias_m, bias_d, bias_mf, bias_df, bias_mo, bias_do, lam, g, out_scale):
    nseq, n_pages = page_table.shape
    n_tok = q.shape[1]
    P = PAGES_PER_STEP if n_pages % PAGES_PER_STEP == 0 else PAGES_PER_STEP // 2
    assert n_pages % P == 0 and n_pages // 2 <= LANES and n_tok % 8 == 0
    n_steps = n_pages // P
    nb = n_pages // 2
    n_tail = bias_m.shape[0]
    rm, rd = MOBA_HEADS * n_tok, DIFF_MAPS * n_tok
    page = lambda j: pl.BlockSpec((1, 1, KV_HEADS, HEAD_DIM, PAGE_SIZE),
                                  lambda b, s, pt, j=j: (layer, pt[b, s * P + j], 0, 0, 0))
    const = lambda a: pl.BlockSpec(a.shape, lambda b, s, pt: (0,) * a.ndim)
    per_seq = lambda a: pl.BlockSpec((1,) + a.shape[1:], lambda b, s, pt: (b,) + (0,) * (a.ndim - 1))
    return pl.pallas_call(
        functools.partial(_sample_attn_kernel, n_tok=n_tok, n_steps=n_steps, n_tail=n_tail, P=P,
                          out_scale=out_scale),
        grid_spec=pltpu.PrefetchScalarGridSpec(
            num_scalar_prefetch=1,
            grid=(nseq, n_steps),
            in_specs=[pl.BlockSpec(memory_space=pltpu.SMEM)]
                     + [page(j) for j in range(P)] + [page(j) for j in range(P)]
                     + [per_seq(q), per_seq(k_new), per_seq(v_new),
                        const(bias_m), const(bias_d), const(bias_mf), const(bias_df), const(bias_mo), const(bias_do),
                        const(g)],
            out_specs=pl.BlockSpec((1, n_tok, QKV_W), lambda b, s, pt: (b, 0, 0)),
            scratch_shapes=[
                pltpu.VMEM((rm, MOBA_W), BF16), pltpu.VMEM((rd, DIFF_W), BF16),
                pltpu.VMEM((rm, LANES), F32), pltpu.VMEM((rm, LANES), F32), pltpu.VMEM((rm, LANES), F32),
                pltpu.VMEM((nb, rm, MOBA_W), F32),
                pltpu.VMEM((rd, 1), F32), pltpu.VMEM((rd, 1), F32), pltpu.VMEM((rd, DIFF_W), F32),
            ],
        ),
        out_shape=jax.ShapeDtypeStruct((nseq, n_tok, QKV_W), F32),
        compiler_params=_cparams(("parallel", "arbitrary")),
        name="sample_attn",
    )(page_table, lam, *([cache_kt] * P), *([cache_vt] * P), q, k_new, v_new,
      bias_m, bias_d, bias_mf, bias_df, bias_mo, bias_do, g)


def _sample_bias_kernel(rowtab_ref, o_ref, *, first_pos, past_len, n_tok):
    n = pl.program_id(0)
    rows, blk = o_ref.shape[1], o_ref.shape[2]
    t = lax.broadcasted_iota(jnp.int32, (rows, blk), 0) % n_tok
    pos = first_pos + n * blk + lax.broadcasted_iota(jnp.int32, (rows, blk), 1)
    o_ref[0] = _bias_of_distance(past_len + t - pos, lambda b: rowtab_ref[:, b:b + 1])


def _sample_bias(bias_table, past_len, n_tok):
    tab = bias_table.astype(F32).T
    tab_m = jnp.repeat(tab[:MOBA_HEADS], n_tok, axis=0)
    tab_d = jnp.repeat(tab[MOBA_HEADS:], 2 * n_tok, axis=0)
    nb = past_len // MOBA_BLOCK
    n_tail = min(nb, BIAS_TAIL_BLOCKS)
    assert n_tail == nb or n_tail * MOBA_BLOCK + 1 >= _bucket_thresholds()[-1]

    def past(rowtab):
        rows = rowtab.shape[0]
        return pl.pallas_call(
            functools.partial(_sample_bias_kernel, first_pos=(nb - n_tail) * MOBA_BLOCK, past_len=past_len,
                              n_tok=n_tok),
            grid=(n_tail,),
            in_specs=[pl.BlockSpec(rowtab.shape, lambda n: (0, 0))],
            out_specs=pl.BlockSpec((1, rows, MOBA_BLOCK), lambda n: (n, 0, 0)),
            out_shape=jax.ShapeDtypeStruct((n_tail, rows, MOBA_BLOCK), F32),
            compiler_params=_cparams(("parallel",)),
            name="sample_bias",
        )(rowtab)

    far = lambda rowtab: jnp.broadcast_to(rowtab[:, NUM_BUCKETS - 1:], (rowtab.shape[0], MOBA_BLOCK))

    t = np.arange(n_tok)
    d_own = t[:, None] - np.arange(LANES)[None, :]
    valid = (d_own >= 0) & (np.arange(LANES)[None, :] < n_tok)

    def own(rowtab):
        reps = rowtab.shape[0] // n_tok
        picked = jnp.take_along_axis(rowtab, jnp.asarray(np.tile(_bucket_np(d_own), (reps, 1))), axis=1)
        return jnp.where(jnp.asarray(np.tile(valid, (reps, 1))), picked, NEG)

    return past(tab_m), past(tab_d), far(tab_m), far(tab_d), own(tab_m), own(tab_d)


def _lambda_init(i):
    return 0.8 - 0.6 * math.exp(-0.3 * i)


def _perm_w_in(w):
    sizes = (MOBA_W, MOBA_W, MOBA_W, DIFF_W, DIFF_W, DIFF_W, GMLP_W, GMLP_W)
    o = np.concatenate([[0], np.cumsum(sizes)])
    qm, km, vm, qd, kd, vd, u, vg = [w[:, o[j]:o[j + 1]] for j in range(8)]
    return jnp.concatenate([qm, qd, km, kd, vm, vd, u, vg], axis=1).astype(BF16)


def _pad_router(w):
    wp = jnp.pad(w.astype(F32), ((0, 0), (0, LANES - N_EXPERTS)))
    return jnp.stack(_split3(wp))


TM_PROMPT, TM_SAMPLE = 512, 256
TMF_PROMPT, TMF_SAMPLE = 512, 128


def _trunk(h, p_layers, mix_fn, W, *, tm, tm_f):
    T = h.shape[0]
    depth = len(W)
    ks, vs, gs = [], [], []
    for i, lw in enumerate(W):
        om, od, og, k_out, v_out, vg = mix_fn(i, lw, h)
        h1, hn, route = _post_a(h, om, od, og, lw["w_out"], lw["g_ffn"], lw["w_router"], tm=tm)
        if lw["moe"]:
            te, nu, slot_token, slot_scale, slot = _moe_dispatch(route, tm=tm_f)
            y = _ffn(te, nu, jnp.take(hn, slot_token, axis=0), lw["wg"], lw["wu"], lw["wd"], slot_scale, tm=tm_f)
            ys = [jnp.take(y, slot[:, j], axis=0) for j in range(TOP_K)]
        else:
            n_tiles = T // tm_f
            ys = [_ffn(jnp.zeros((n_tiles,), jnp.int32), jnp.full((1,), n_tiles, jnp.int32), hn,
                       lw["wg"], lw["wu"], lw["wd"], jnp.ones((T, 1), F32), tm=tm_f)]
        h = _post_b(h1, ys, p_layers[i].reshape(T, D_PLE), lw["g_ple"], lw["w_ple_gate"], lw["w_ple_proj"],
                    lw["g_final"], tm=tm, final=(i == depth - 1))
        ks.append(k_out)
        vs.append(v_out)
        gs.append(vg)
    return h, ks, vs, gs


def kernel(x_prompt, x_sample, cache_k, cache_v, page_table, p_prompt, p_sample, bias_table, norm_mix, w_in, w_out, lambda_q1, lambda_k1, lambda_q2, lambda_k2, diff_norm, gmlp_w_s, gmlp_b_s, norm_ffn, w_ffn_gate, w_ffn_up, w_ffn_down, w_router, w_exp_gate, w_exp_up, w_exp_down, norm_ple, w_ple_gate, w_ple_proj, norm_final):
    B, S, _ = x_prompt.shape
    nseq, n_tok, _ = x_sample.shape
    depth = cache_k.shape[0]
    n_pages = page_table.shape[1]
    past_len = n_pages * PAGE_SIZE
    nblk = S // MOBA_BLOCK
    assert S % TM_PROMPT == 0 and nblk <= LANES and past_len % MOBA_BLOCK == 0
    assert GMLP_CHUNK % n_tok == 0 and (nseq * n_tok) % TM_SAMPLE == 0

    row2 = lambda a: a.astype(F32).reshape(1, -1)
    q_scale = jnp.asarray(np.concatenate([np.full(MOBA_W, HEAD_DIM ** -0.5, np.float32),
                                          np.full(DIFF_W, DIFF_QK ** -0.5, np.float32)])[None])
    tril = jnp.asarray(np.tril(np.ones((GMLP_CHUNK, GMLP_CHUNK), np.float32)))
    reps = GMLP_CHUNK // n_tok
    W = []
    for i in range(depth):
        ws = gmlp_w_s[i].astype(F32) * tril
        bs = gmlp_b_s[i].astype(F32)
        mix_prompt = (ws.astype(BF16), jnp.repeat(bs.T, HEAD_DIM, axis=1))
        w_small = ws[:, :n_tok, :n_tok]
        w_blockdiag = jnp.einsum("ab,gts->gatbs", jnp.eye(reps, dtype=F32), w_small).reshape(
            GMLP_GROUPS, GMLP_CHUNK, GMLP_CHUNK)
        mix_sample = (w_blockdiag.astype(BF16), jnp.tile(jnp.repeat(bs[:, :n_tok].T, HEAD_DIM, axis=1), (reps, 1)))
        lam = (jnp.exp(jnp.sum(lambda_q1[i].astype(F32) * lambda_k1[i].astype(F32)))
               - jnp.exp(jnp.sum(lambda_q2[i].astype(F32) * lambda_k2[i].astype(F32))) + _lambda_init(i))
        moe = i % 2 == 1
        j = i // 2
        w_perm = _perm_w_in(w_in[i])
        W.append(dict(
            g_mix=row2(norm_mix[i]), w_in=w_perm,
            w_in_t=w_perm[:, :3 * QKV_W].T.reshape(3, QKV_W, D_MODEL),
            w_in_kuv=jnp.concatenate([w_perm[:, QKV_W:2 * QKV_W], w_perm[:, 3 * QKV_W:]], axis=1),
            mix_prompt=mix_prompt, mix_sample=mix_sample,
            w_out=w_out[i].astype(BF16), g_ffn=row2(norm_ffn[i]),
            w_router=_pad_router(w_router[j]) if moe else None, moe=moe,
            wg=(w_exp_gate[j] if moe else w_ffn_gate[j][None]).astype(BF16),
            wu=(w_exp_up[j] if moe else w_ffn_up[j][None]).astype(BF16),
            wd=(w_exp_down[j] if moe else w_ffn_down[j][None]).astype(BF16),
            g_ple=row2(norm_ple[i]), w_ple_gate=w_ple_gate[i].astype(BF16), w_ple_proj=w_ple_proj[i].astype(BF16),
            g_final=row2(norm_final), lam=lam.reshape(1, 1).astype(F32),
            g_diff=diff_norm[i].astype(F32), out_scale=1.0 - _lambda_init(i),
        ))

    log2e = math.log2(math.e)
    bias_tiles = _prompt_bias_tiles(bias_table.astype(F32) * log2e).reshape(
        KV_HEADS // 2, 2, N_BIAS_TILES, MOBA_BLOCK, MOBA_BLOCK)
    q_scale_prompt = (q_scale * log2e).T
    nbp = -(-nblk // 16) * 16

    def prompt_mix(i, lw, h):
        qt, kb, kt, vt, vtb, og, km = _inproj_prompt(h, lw["g_mix"], lw["w_in_t"], lw["w_in_kuv"], q_scale_prompt,
                                                     *lw["mix_prompt"], B=B, S=S, tm=TM_PROMPT)
        k3 = kb.reshape(B, S, QKV_W)
        kmean = jnp.pad(km.reshape(B, nblk, QKV_W), ((0, 0), (0, nbp - nblk), (0, 0)))
        om = _moba_prompt(qt, k3, vtb, kmean, bias_tiles)
        od = _diff_prompt(qt, k3, vtb, bias_tiles, lw["lam"], lw["g_diff"].reshape(DIFF_HEADS // 2, LANES, 1),
                          lw["out_scale"])
        return om.reshape(B * S, MOBA_W), od.reshape(B * S, DIFF_W), og, kt, vt, None

    y_p, k_p, v_p, _ = _trunk(x_prompt.reshape(B * S, D_MODEL), p_prompt, prompt_mix, W,
                              tm=TM_PROMPT, tm_f=TMF_PROMPT)

    cache_kt = jnp.transpose(cache_k, (0, 1, 3, 4, 2))
    cache_vt = jnp.transpose(cache_v, (0, 1, 3, 4, 2))
    sbias = _sample_bias(bias_table, past_len, n_tok)

    def sample_mix(i, lw, h):
        q, kf, vf, og, vg = _inproj_sample(h, lw["g_mix"], lw["w_in"], q_scale, *lw["mix_sample"], tm=TM_SAMPLE)
        pad_new = lambda a: jnp.pad(a.reshape(nseq, n_tok, QKV_W), ((0, 0), (0, LANES - n_tok), (0, 0)))
        o = _sample_attn(i, page_table, cache_kt, cache_vt, q.reshape(nseq, n_tok, QKV_W),
                         pad_new(kf), pad_new(vf), *sbias, lw["lam"], lw["g_diff"].reshape(1, DIFF_W),
                         lw["out_scale"])
        o = o.reshape(nseq * n_tok, QKV_W).astype(BF16)
        return o[:, :MOBA_W], o[:, MOBA_W:], og, kf, vf, vg

    y_s, k_s, v_s, g_s = _trunk(x_sample.reshape(nseq * n_tok, D_MODEL), p_sample, sample_mix, W,
                                tm=TM_SAMPLE, tm_f=TMF_SAMPLE)

    heads_t = lambda cols: jnp.transpose(
        jnp.stack(cols).reshape(depth, B, KV_HEADS, HEAD_DIM, S), (0, 1, 4, 2, 3))
    heads = lambda rows: jnp.stack(rows).reshape(depth, nseq, n_tok, KV_HEADS, HEAD_DIM)
    return (y_p.reshape(B, S, D_MODEL), y_s.reshape(nseq, n_tok, D_MODEL),
            heads_t(k_p), heads_t(v_p), heads(k_s), heads(v_s),
            jnp.stack(g_s).reshape(depth, nseq, n_tok, GMLP_W))
```

```python
import functools
import math

import jax
import jax.numpy as jnp
import numpy as np
from jax import lax
from jax.experimental import pallas as pl
from jax.experimental.pallas import tpu as pltpu

F32 = jnp.float32
BF16 = jnp.bfloat16

D_MODEL = 1024
HEAD_DIM = 64
MOBA_HEADS = 6
DIFF_HEADS = 4
KV_HEADS = MOBA_HEADS + DIFF_HEADS
MOBA_W = MOBA_HEADS * HEAD_DIM
DIFF_W = DIFF_HEADS * HEAD_DIM
GMLP_GROUPS = 6
GMLP_W = GMLP_GROUPS * HEAD_DIM
QKV_W = MOBA_W + DIFF_W
DIFF_QK = HEAD_DIM // 2
MOBA_BLOCK = 256
MOBA_TOPK = 3
GMLP_CHUNK = 128
NUM_BUCKETS = 32
MAX_EXACT = 16
MAX_DISTANCE = 2048
D_FF = 2816
N_EXPERTS = 8
TOP_K = 2
D_PLE = 256
PAGE_SIZE = 128
RMS_EPS = 1e-6

LANES = 128
NEG = -1e30
FF_CHUNK = D_FF // 2
N_BIAS_TILES = 8
VMEM_LIMIT = 56 * 1024 * 1024


def _cparams(sem):
    return pltpu.CompilerParams(dimension_semantics=sem, vmem_limit_bytes=VMEM_LIMIT)


def _rms(x, g):
    return x * lax.rsqrt(jnp.mean(x * x, axis=-1, keepdims=True) + RMS_EPS) * g


def _bucket_np(dist):
    n = np.maximum(dist, 0)
    nf = np.maximum(n, MAX_EXACT).astype(np.float32)
    large = MAX_EXACT + (np.log(nf / np.float32(MAX_EXACT)) / np.float32(math.log(MAX_DISTANCE / MAX_EXACT))
                         * np.float32(NUM_BUCKETS - MAX_EXACT)).astype(np.int32)
    return np.where(n < MAX_EXACT, n, np.minimum(large, NUM_BUCKETS - 1)).astype(np.int32)


def _bucket_thresholds():
    buckets = _bucket_np(np.arange(2 * MAX_DISTANCE))
    assert np.all(np.diff(buckets) >= 0) and buckets[-1] == NUM_BUCKETS - 1
    return [int(np.argmax(buckets >= b)) for b in range(NUM_BUCKETS)]


def _bias_of_distance(dist, value_of_bucket):
    thr = _bucket_thresholds()
    out = jnp.zeros(dist.shape, F32) + value_of_bucket(0)
    for b in range(1, NUM_BUCKETS):
        out = jnp.where(dist >= thr[b], value_of_bucket(b), out)
    return out


def _prompt_bias_kernel(tab_ref, o_ref):
    h, d = pl.program_id(0), pl.program_id(1)
    blk = MOBA_BLOCK
    kj = lax.broadcasted_iota(jnp.int32, (blk, blk), 0)
    qi = lax.broadcasted_iota(jnp.int32, (blk, blk), 1)
    dist = qi - kj + blk * d
    bias = _bias_of_distance(dist, lambda b: tab_ref[h, b])
    o_ref[0, 0] = jnp.where(dist >= 0, bias, NEG)


def _prompt_bias_tiles(bias_table):
    assert MOBA_BLOCK * (N_BIAS_TILES - 2) + 1 >= _bucket_thresholds()[-1]
    blk = MOBA_BLOCK
    return pl.pallas_call(
        _prompt_bias_kernel,
        grid=(KV_HEADS, N_BIAS_TILES),
        in_specs=[pl.BlockSpec(memory_space=pltpu.SMEM)],
        out_specs=pl.BlockSpec((1, 1, blk, blk), lambda h, d: (h, d, 0, 0)),
        out_shape=jax.ShapeDtypeStruct((KV_HEADS, N_BIAS_TILES, blk, blk), F32),
        compiler_params=_cparams(("parallel", "parallel")),
        name="prompt_bias",
    )(bias_table.astype(F32).T)


def _gmlp_tile(u, vg, ws_ref, bs_ref, og_ref, tm):
    vgb = vg.astype(BF16)
    first = lax.broadcasted_iota(jnp.int32, (1, LANES), 1) < HEAD_DIM
    for c in range(tm // GMLP_CHUNK):
        rows = slice(c * GMLP_CHUNK, (c + 1) * GMLP_CHUNK)
        for p in range(GMLP_GROUPS // 2):
            cols = slice(p * LANES, (p + 1) * LANES)
            vc = vgb[rows, cols]
            m0 = jnp.dot(ws_ref[2 * p], vc, preferred_element_type=F32)
            m1 = jnp.dot(ws_ref[2 * p + 1], vc, preferred_element_type=F32)
            mixed = jnp.where(first, m0, m1) + bs_ref[:, cols]
            og_ref[rows, cols] = (u[rows, cols] * mixed).astype(BF16)


def _inproj_prompt_kernel(x_ref, g_ref, wt_ref, w_ref, qs_ref, ws_ref, bs_ref,
                          qt_ref, kb_ref, kt_ref, vt_ref, vtb_ref, og_ref, km_ref, *, tm):
    xn = _rms(x_ref[...], g_ref[...]).astype(BF16)
    proj_t = lambda wt: lax.dot_general(wt, xn, (((1,), (1,)), ((), ())), preferred_element_type=F32)
    qt_ref[0] = (proj_t(wt_ref[0]) * qs_ref[...]).astype(BF16)
    kt_ref[0] = proj_t(wt_ref[1])
    vt = proj_t(wt_ref[2])
    vt_ref[0] = vt
    k = jnp.dot(xn, w_ref[:, :QKV_W], preferred_element_type=F32)
    kb_ref[...] = k.astype(BF16)
    for j in range(tm // MOBA_BLOCK):
        cols = slice(j * MOBA_BLOCK, (j + 1) * MOBA_BLOCK)
        vtb_ref[0, j] = vt[:, cols].astype(BF16)
        km_ref[j] = jnp.mean(k[cols], axis=0, keepdims=True)
    u = jax.nn.gelu(jnp.dot(xn, w_ref[:, QKV_W:QKV_W + GMLP_W], preferred_element_type=F32))
    vg = jax.nn.gelu(jnp.dot(xn, w_ref[:, QKV_W + GMLP_W:], preferred_element_type=F32))
    _gmlp_tile(u, vg, ws_ref, bs_ref, og_ref, tm)


def _inproj_prompt(x, g, w_t, w_kuv, q_scale_col, w_mix, b_mix, *, B, S, tm):
    assert S % tm == 0 and tm % MOBA_BLOCK == 0
    nj = S // tm
    nkm = tm // MOBA_BLOCK
    row = lambda w: pl.BlockSpec((tm, w), lambda b, j: (b * nj + j, 0))
    full = lambda a: pl.BlockSpec(a.shape, lambda b, j: (0,) * a.ndim)
    tcol = pl.BlockSpec((1, QKV_W, tm), lambda b, j: (b, 0, j))
    out_shape = (
        jax.ShapeDtypeStruct((B, QKV_W, S), BF16),
        jax.ShapeDtypeStruct((B * S, QKV_W), BF16),
        jax.ShapeDtypeStruct((B, QKV_W, S), F32),
        jax.ShapeDtypeStruct((B, QKV_W, S), F32),
        jax.ShapeDtypeStruct((B, S // MOBA_BLOCK, QKV_W, MOBA_BLOCK), BF16),
        jax.ShapeDtypeStruct((B * S, GMLP_W), BF16),
        jax.ShapeDtypeStruct((B * S // MOBA_BLOCK, 1, QKV_W), F32),
    )
    out_specs = (tcol, row(QKV_W), tcol, tcol,
                 pl.BlockSpec((1, nkm, QKV_W, MOBA_BLOCK), lambda b, j: (b, j, 0, 0)),
                 row(GMLP_W),
                 pl.BlockSpec((nkm, 1, QKV_W), lambda b, j: (b * nj + j, 0, 0)))
    return pl.pallas_call(
        functools.partial(_inproj_prompt_kernel, tm=tm),
        grid=(B, nj),
        in_specs=[row(D_MODEL), full(g), full(w_t), full(w_kuv), full(q_scale_col), full(w_mix), full(b_mix)],
        out_specs=out_specs,
        out_shape=out_shape,
        compiler_params=_cparams(("parallel", "parallel")),
        name="inproj_prompt",
    )(x, g, w_t, w_kuv, q_scale_col, w_mix, b_mix)


def _inproj_sample_kernel(x_ref, g_ref, w_ref, qs_ref, ws_ref, bs_ref, q_ref, kf_ref, vf_ref, og_ref, vg_ref, *, tm):
    xn = _rms(x_ref[...], g_ref[...]).astype(BF16)
    c1, c2, c3, c4 = QKV_W, 2 * QKV_W, 3 * QKV_W, 3 * QKV_W + GMLP_W
    q_ref[...] = jnp.dot(xn, w_ref[:, :c1], preferred_element_type=F32) * qs_ref[...]
    kf_ref[...] = jnp.dot(xn, w_ref[:, c1:c2], preferred_element_type=F32)
    vf_ref[...] = jnp.dot(xn, w_ref[:, c2:c3], preferred_element_type=F32)
    u = jax.nn.gelu(jnp.dot(xn, w_ref[:, c3:c4], preferred_element_type=F32))
    vg = jax.nn.gelu(jnp.dot(xn, w_ref[:, c4:], preferred_element_type=F32))
    vg_ref[...] = vg
    _gmlp_tile(u, vg, ws_ref, bs_ref, og_ref, tm)


def _inproj_sample(x, g, w_perm, q_scale, w_mix, b_mix, *, tm):
    T = x.shape[0]
    assert T % tm == 0 and tm % GMLP_CHUNK == 0
    row = lambda w: pl.BlockSpec((tm, w), lambda i: (i, 0))
    full = lambda a: pl.BlockSpec(a.shape, lambda i: (0,) * a.ndim)
    out_shape = (
        jax.ShapeDtypeStruct((T, QKV_W), F32),
        jax.ShapeDtypeStruct((T, QKV_W), F32),
        jax.ShapeDtypeStruct((T, QKV_W), F32),
        jax.ShapeDtypeStruct((T, GMLP_W), BF16),
        jax.ShapeDtypeStruct((T, GMLP_W), F32),
    )
    return pl.pallas_call(
        functools.partial(_inproj_sample_kernel, tm=tm),
        grid=(T // tm,),
        in_specs=[row(D_MODEL), full(g), full(w_perm), full(q_scale), full(w_mix), full(b_mix)],
        out_specs=(row(QKV_W), row(QKV_W), row(QKV_W), row(GMLP_W), row(GMLP_W)),
        out_shape=out_shape,
        compiler_params=_cparams(("parallel",)),
        name="inproj_sample",
    )(x, g, w_perm, q_scale, w_mix, b_mix)


def _masked_queries(qt, group_width, n_maps):
    sub = lax.broadcasted_iota(jnp.int32, (LANES, 1), 0)
    zero = jnp.zeros_like(qt)
    return jnp.concatenate([jnp.where(sub // group_width == j, qt, zero) for j in range(n_maps)], axis=1)


def _softmax_probs(s, m_ref, l_ref, p_ref, alpha_ref):
    blk = MOBA_BLOCK
    for c in range(s.shape[1] // blk):
        cols = slice(c * blk, (c + 1) * blk)
        sc = s[:, cols]
        m_old = m_ref[:, cols]
        m = jnp.maximum(m_old, jnp.max(sc, axis=0, keepdims=True))
        alpha = jnp.exp2(m_old - m)
        p = jnp.exp2(sc - m)
        m_ref[:, cols] = m
        l_ref[:, cols] = alpha * l_ref[:, cols] + jnp.sum(p, axis=0, keepdims=True)
        alpha_ref[:, cols] = alpha
        p_ref[:, cols] = p.astype(BF16)


def _accumulate_values(vt, p_ref, alpha_ref, acc_ref):
    blk = MOBA_BLOCK
    n_maps = p_ref.shape[1] // blk
    for c in range(n_maps):
        cols = slice(c * blk, (c + 1) * blk)
        head = c * 2 // n_maps
        pv = jnp.dot(vt[head * HEAD_DIM:(head + 1) * HEAD_DIM], p_ref[:, cols], preferred_element_type=F32)
        acc_ref[:, cols] = alpha_ref[:, cols] * acc_ref[:, cols] + pv


def _attend_blocks(i, k_ref, vt_ref, qw_ref, s_refs, p_refs, alpha_refs, m_ref, l_ref, acc_ref, bias, mask):
    blk = MOBA_BLOCK

    def scores(n, dst):
        kb = k_ref[0, pl.ds(pl.multiple_of(jnp.minimum(n, i) * blk, blk), blk), :]
        dst[...] = jnp.dot(kb, qw_ref[...], preferred_element_type=F32)

    def stage(n, x):
        scores(n + 1, s_refs[1 - x])
        s = s_refs[x][...] + bias(jnp.minimum(i - n, N_BIAS_TILES - 1))
        _softmax_probs(mask(s, n), m_ref, l_ref, p_refs[x], alpha_refs[x])
        _accumulate_values(vt_ref[0, jnp.maximum(n - 1, 0)], p_refs[1 - x], alpha_refs[1 - x], acc_ref)

    m_ref[...] = jnp.full(m_ref.shape, NEG, F32)
    l_ref[...] = jnp.zeros(l_ref.shape, F32)
    acc_ref[...] = jnp.zeros(acc_ref.shape, F32)
    p_refs[1][...] = jnp.zeros(p_refs[1].shape, BF16)
    alpha_refs[1][...] = jnp.ones(alpha_refs[1].shape, F32)
    scores(0, s_refs[0])

    def body(j, carry):
        n = 2 * j
        stage(n, 0)

        @pl.when(n + 1 <= i)
        def _():
            stage(n + 1, 1)

        return carry

    lax.fori_loop(0, (i + 2) // 2, body, 0)
    for x in range(2):
        @pl.when(i % 2 == x)
        def _():
            _accumulate_values(vt_ref[0, i], p_refs[x], alpha_refs[x], acc_ref)


def _qk(qm, k):
    return lax.dot_general(qm, k, (((1,), (1,)), ((), ())), preferred_element_type=F32)


def _split3(x):
    hi = x.astype(BF16)
    r = x - hi.astype(F32)
    mid = r.astype(BF16)
    lo = (r - mid.astype(F32)).astype(BF16)
    return hi, mid, lo


def _top3_mask(gate, pos, axis):
    sel = jnp.zeros(gate.shape, F32)
    for _ in range(MOBA_TOPK):
        m = jnp.max(gate, axis=axis, keepdims=True)
        idx = jnp.min(jnp.where(gate == m, pos, gate.shape[axis]), axis=axis, keepdims=True)
        hit = pos == idx
        sel = jnp.where(hit, 1.0, sel)
        gate = jnp.where(hit, -jnp.inf, gate)
    return sel


def _moba_prompt_kernel(qt_ref, k_ref, vt_ref, km_ref, bias_ref, o_ref,
                        qw_ref, s0_ref, s1_ref, p0_ref, p1_ref, a0_ref, a1_ref, m_ref, l_ref, acc_ref, sel_ref):
    i = pl.program_id(2)
    blk = MOBA_BLOCK
    dot = lambda a, b: jnp.dot(a, b, preferred_element_type=F32)
    qw_ref[...] = _masked_queries(qt_ref[0], HEAD_DIM, 2)
    qw = qw_ref[...]
    blk_id = lax.broadcasted_iota(jnp.int32, sel_ref.shape, 0)
    km_hi, km_mid, km_lo = _split3(km_ref[0])
    gate = dot(km_hi, qw) + dot(km_mid, qw) + dot(km_lo, qw)
    gate = jnp.where(blk_id < i, gate, -jnp.inf)
    sel_ref[...] = jnp.where(blk_id < i, _top3_mask(gate, blk_id, 0), jnp.where(blk_id == i, 1.0, 0.0))
    bias = lambda d: jnp.concatenate([bias_ref[0, 0, d], bias_ref[0, 1, d]], axis=1)
    mask = lambda s, n: jnp.where(sel_ref[pl.ds(n, 1), :] > 0.5, s, NEG)
    _attend_blocks(i, k_ref, vt_ref, qw_ref, (s0_ref, s1_ref), (p0_ref, p1_ref), (a0_ref, a1_ref),
                   m_ref, l_ref, acc_ref, bias, mask)
    o = acc_ref[...] / l_ref[...]
    o_ref[0] = jnp.concatenate([o[:, :blk], o[:, blk:]], axis=0).T.astype(BF16)


def _prompt_attn_specs(S, nbp, off):
    blk = MOBA_BLOCK
    return dict(
        qt=pl.BlockSpec((1, LANES, blk), lambda b, p, i: (b, p + off, i)),
        k=pl.BlockSpec((1, S, LANES), lambda b, p, i: (b, 0, p + off)),
        vt=pl.BlockSpec((1, S // blk, LANES, blk), lambda b, p, i: (b, 0, p + off, 0)),
        km=pl.BlockSpec((1, nbp, LANES), lambda b, p, i: (b, 0, p + off)),
        bias=pl.BlockSpec((1, 2, N_BIAS_TILES, blk, blk), lambda b, p, i: (p + off, 0, 0, 0, 0)),
        out=pl.BlockSpec((1, blk, LANES), lambda b, p, i: (b, i, p)),
    )


def _moba_prompt(qt, k, vtb, kmean, bias_tiles):
    B, S, _ = k.shape
    blk = MOBA_BLOCK
    nbp = kmean.shape[1]
    sp = _prompt_attn_specs(S, nbp, 0)
    return pl.pallas_call(
        _moba_prompt_kernel,
        grid=(B, MOBA_HEADS // 2, S // blk),
        in_specs=[sp["qt"], sp["k"], sp["vt"], sp["km"], sp["bias"]],
        out_specs=sp["out"],
        out_shape=jax.ShapeDtypeStruct((B, S, MOBA_W), BF16),
        scratch_shapes=[pltpu.VMEM((LANES, 2 * blk), BF16),
                        pltpu.VMEM((blk, 2 * blk), F32), pltpu.VMEM((blk, 2 * blk), F32),
                        pltpu.VMEM((blk, 2 * blk), BF16), pltpu.VMEM((blk, 2 * blk), BF16),
                        pltpu.VMEM((1, 2 * blk), F32), pltpu.VMEM((1, 2 * blk), F32),
                        pltpu.VMEM((1, 2 * blk), F32), pltpu.VMEM((1, 2 * blk), F32),
                        pltpu.VMEM((HEAD_DIM, 2 * blk), F32), pltpu.VMEM((nbp, 2 * blk), F32)],
        compiler_params=_cparams(("parallel", "parallel", "arbitrary")),
        name="moba_prompt",
    )(qt, k, vtb, kmean, bias_tiles)


def _diff_prompt_kernel(lam_ref, qt_ref, k_ref, vt_ref, bias_ref, g_ref, o_ref,
                        qw_ref, s0_ref, s1_ref, p0_ref, p1_ref, a0_ref, a1_ref, m_ref, l_ref, acc_ref, *, out_scale):
    i = pl.program_id(2)
    blk = MOBA_BLOCK
    qw_ref[...] = _masked_queries(qt_ref[0], DIFF_QK, 4)
    bias = lambda d: jnp.concatenate([bias_ref[0, j // 2, d] for j in range(4)], axis=1)
    _attend_blocks(i, k_ref, vt_ref, qw_ref, (s0_ref, s1_ref), (p0_ref, p1_ref), (a0_ref, a1_ref),
                   m_ref, l_ref, acc_ref, bias, lambda s, n: s)
    lam = lam_ref[0, 0]
    o = acc_ref[...] / l_ref[...]
    outs = []
    for h in range(2):
        oh = o[:, (2 * h) * blk:(2 * h + 1) * blk] - lam * o[:, (2 * h + 1) * blk:(2 * h + 2) * blk]
        outs.append(oh * lax.rsqrt(jnp.mean(oh * oh, axis=0, keepdims=True) + RMS_EPS))
    ot = jnp.concatenate(outs, axis=0) * g_ref[0] * out_scale
    o_ref[0] = ot.T.astype(BF16)


def _diff_prompt(qt, k, vtb, bias_tiles, lam, g_col, out_scale):
    B, S, _ = k.shape
    blk = MOBA_BLOCK
    sp = _prompt_attn_specs(S, 0, MOBA_HEADS // 2)
    return pl.pallas_call(
        functools.partial(_diff_prompt_kernel, out_scale=out_scale),
        grid=(B, DIFF_HEADS // 2, S // blk),
        in_specs=[pl.BlockSpec(memory_space=pltpu.SMEM), sp["qt"], sp["k"], sp["vt"], sp["bias"],
                  pl.BlockSpec((1, LANES, 1), lambda b, p, i: (p, 0, 0))],
        out_specs=sp["out"],
        out_shape=jax.ShapeDtypeStruct((B, S, DIFF_W), BF16),
        scratch_shapes=[pltpu.VMEM((LANES, 4 * blk), BF16),
                        pltpu.VMEM((blk, 4 * blk), F32), pltpu.VMEM((blk, 4 * blk), F32),
                        pltpu.VMEM((blk, 4 * blk), BF16), pltpu.VMEM((blk, 4 * blk), BF16),
                        pltpu.VMEM((1, 4 * blk), F32), pltpu.VMEM((1, 4 * blk), F32),
                        pltpu.VMEM((1, 4 * blk), F32), pltpu.VMEM((1, 4 * blk), F32),
                        pltpu.VMEM((HEAD_DIM, 4 * blk), F32)],
        compiler_params=_cparams(("parallel", "parallel", "arbitrary")),
        name="diff_prompt",
    )(lam, qt, k, vtb, bias_tiles, g_col)


def _post_a_kernel(x_ref, om_ref, od_ref, og_ref, wo_ref, g_ref, *rest):
    h = x_ref[...]
    h = h + jnp.dot(om_ref[...], wo_ref[0:MOBA_W, :], preferred_element_type=F32)
    h = h + jnp.dot(od_ref[...], wo_ref[MOBA_W:QKV_W, :], preferred_element_type=F32)
    h = h + jnp.dot(og_ref[...], wo_ref[QKV_W:, :], preferred_element_type=F32)
    if len(rest) == 2:
        h_ref, hn_ref = rest
    else:
        wr_ref, h_ref, hn_ref, route_ref = rest
    h_ref[...] = h
    hn = _rms(h, g_ref[...])
    hn_ref[...] = hn.astype(BF16)
    if len(rest) == 2:
        return
    a_hi, a_mid, _ = _split3(hn)
    dot = lambda a, b: jnp.dot(a, b, preferred_element_type=F32)
    logits = dot(a_hi, wr_ref[0]) + (dot(a_hi, wr_ref[1]) + dot(a_mid, wr_ref[0]))
    col = lax.broadcasted_iota(jnp.int32, logits.shape, 1)
    logits = jnp.where(col < N_EXPERTS, logits, -jnp.inf)
    m1 = jnp.max(logits, axis=1, keepdims=True)
    i1 = jnp.min(jnp.where(logits == m1, col, LANES), axis=1, keepdims=True)
    rest = jnp.where(col == i1, -jnp.inf, logits)
    m2 = jnp.max(rest, axis=1, keepdims=True)
    i2 = jnp.min(jnp.where(rest == m2, col, LANES), axis=1, keepdims=True)
    e = jnp.exp(m2 - m1)
    w1 = 1.0 / (1.0 + e)
    w2 = e / (1.0 + e)
    route = jnp.where(col == 0, i1.astype(F32), 0.0)
    route = jnp.where(col == 1, i2.astype(F32), route)
    route = jnp.where(col == 2, w1, route)
    route = jnp.where(col == 3, w2, route)
    route_ref[...] = route


def _post_a(x, om, od, og, w_out, g, w_router3, *, tm):
    T = x.shape[0]
    row = lambda w: pl.BlockSpec((tm, w), lambda i: (i, 0))
    full = lambda a: pl.BlockSpec(a.shape, lambda i: (0,) * a.ndim)
    routed = w_router3 is not None
    outs = pl.pallas_call(
        _post_a_kernel,
        grid=(T // tm,),
        in_specs=[row(D_MODEL), row(MOBA_W), row(DIFF_W), row(GMLP_W), full(w_out), full(g)]
                 + ([full(w_router3)] if routed else []),
        out_specs=(row(D_MODEL), row(D_MODEL)) + ((row(LANES),) if routed else ()),
        out_shape=(jax.ShapeDtypeStruct((T, D_MODEL), F32), jax.ShapeDtypeStruct((T, D_MODEL), BF16))
                  + ((jax.ShapeDtypeStruct((T, LANES), F32),) if routed else ()),
        compiler_params=_cparams(("parallel",)),
        name="post_a",
    )(x, om, od, og, w_out, g, *([w_router3] if routed else []))
    return outs if routed else (*outs, None)


def _ffn_kernel(te_ref, nu_ref, x_ref, wg_ref, wu_ref, wd_ref, sc_ref, o_ref, acc_ref):
    t = pl.program_id(0)
    f = pl.program_id(1)

    @pl.when(t < nu_ref[0])
    def _():
        x = x_ref[...]
        g = jnp.dot(x, wg_ref[0], preferred_element_type=F32)
        u = jnp.dot(x, wu_ref[0], preferred_element_type=F32)
        a = (g * jax.nn.sigmoid(g) * u).astype(BF16)
        part = jnp.dot(a, wd_ref[0], preferred_element_type=F32)

        @pl.when(f == 0)
        def _():
            acc_ref[...] = part

        @pl.when(f == 1)
        def _():
            o_ref[...] = (acc_ref[...] + part) * sc_ref[...]

    @pl.when(jnp.logical_and(t >= nu_ref[0], f == 1))
    def _():
        o_ref[...] = jnp.zeros_like(o_ref)


def _ffn(tile_expert, n_used, x, wg, wu, wd, scale, *, tm):
    N = x.shape[0]
    n_tiles = N // tm
    chunk = lambda t, f: jnp.where(t % 2 == 0, f, 1 - f)
    return pl.pallas_call(
        _ffn_kernel,
        grid_spec=pltpu.PrefetchScalarGridSpec(
            num_scalar_prefetch=2,
            grid=(n_tiles, 2),
            in_specs=[
                pl.BlockSpec((tm, D_MODEL), lambda t, f, te, nu: (t, 0)),
                pl.BlockSpec((1, D_MODEL, FF_CHUNK), lambda t, f, te, nu: (te[t], 0, chunk(t, f))),
                pl.BlockSpec((1, D_MODEL, FF_CHUNK), lambda t, f, te, nu: (te[t], 0, chunk(t, f))),
                pl.BlockSpec((1, FF_CHUNK, D_MODEL), lambda t, f, te, nu: (te[t], chunk(t, f), 0)),
                pl.BlockSpec((tm, 1), lambda t, f, te, nu: (t, 0)),
            ],
            out_specs=pl.BlockSpec((tm, D_MODEL), lambda t, f, te, nu: (t, 0)),
            scratch_shapes=[pltpu.VMEM((tm, D_MODEL), F32)],
        ),
        out_shape=jax.ShapeDtypeStruct((N, D_MODEL), F32),
        compiler_params=_cparams(("arbitrary", "arbitrary")),
        name="ffn",
    )(tile_expert, n_used, x, wg, wu, wd, scale)


def _moe_dispatch(route, *, tm):
    T = route.shape[0]
    n_assign = T * TOP_K
    n_tiles = -(-n_assign // tm) + N_EXPERTS
    experts = route[:, :TOP_K].astype(jnp.int32).reshape(n_assign)
    weights = route[:, TOP_K:2 * TOP_K].reshape(n_assign)
    onehot = (experts[:, None] == jnp.arange(N_EXPERTS)[None, :]).astype(jnp.int32)
    rank = jnp.sum((jnp.cumsum(onehot, axis=0) - onehot) * onehot, axis=1)
    counts = jnp.sum(onehot, axis=0)
    tiles_per = (counts + tm - 1) // tm
    tile_end = jnp.cumsum(tiles_per)
    slot = ((tile_end - tiles_per) * tm)[experts] + rank
    tile_expert = jnp.minimum(jnp.searchsorted(tile_end, jnp.arange(n_tiles), side="right"),
                              N_EXPERTS - 1).astype(jnp.int32)
    n_used = tile_end[-1:].astype(jnp.int32)
    token = jnp.arange(n_assign, dtype=jnp.int32) // TOP_K
    slot_token = jnp.zeros((n_tiles * tm,), jnp.int32).at[slot].set(token)
    slot_scale = jnp.zeros((n_tiles * tm,), F32).at[slot].set(weights)
    return tile_expert, n_used, slot_token, slot_scale[:, None], slot.reshape(T, TOP_K)


def _post_b_kernel(*refs, n_y, final):
    h_ref = refs[0]
    y_refs = refs[1:1 + n_y]
    p_ref, gp_ref, wgate_ref, wproj_ref, gf_ref, o_ref = refs[1 + n_y:]
    h = h_ref[...]
    for y_ref in y_refs:
        h = h + y_ref[...]
    gate = jax.nn.sigmoid(jnp.dot(_rms(h, gp_ref[...]).astype(BF16), wgate_ref[...], preferred_element_type=F32))
    h = h + gate * jnp.dot(p_ref[...].astype(BF16), wproj_ref[...], preferred_element_type=F32)
    o_ref[...] = _rms(h, gf_ref[...]) if final else h


def _post_b(h, ys, p, g_ple, w_gate, w_proj, g_final, *, tm, final):
    T = h.shape[0]
    row = lambda w: pl.BlockSpec((tm, w), lambda i: (i, 0))
    full = lambda a: pl.BlockSpec(a.shape, lambda i: (0,) * a.ndim)
    return pl.pallas_call(
        functools.partial(_post_b_kernel, n_y=len(ys), final=final),
        grid=(T // tm,),
        in_specs=[row(D_MODEL)] + [row(D_MODEL)] * len(ys)
                 + [row(D_PLE), full(g_ple), full(w_gate), full(w_proj), full(g_final)],
        out_specs=row(D_MODEL),
        out_shape=jax.ShapeDtypeStruct((T, D_MODEL), F32),
        compiler_params=_cparams(("parallel",)),
        name="post_b",
    )(h, *ys, p, g_ple, w_gate, w_proj, g_final)


PAGES_PER_STEP = 16
BIAS_TAIL_BLOCKS = 8
DIFF_MAPS = 2 * DIFF_HEADS


def _sample_attn_kernel(pt_ref, lam_ref, *refs, n_tok, n_steps, n_tail, P, out_scale):
    k_refs = refs[:P]
    v_refs = refs[P:2 * P]
    (q_ref, kn_ref, vn_ref, bm_ref, bd_ref, bmf_ref, bdf_ref, bmo_ref, bdo_ref, g_ref, o_ref,
     qm_ref, qd_ref, gate_ref, mb_ref, lb_ref, accb_ref, m_ref, l_ref, acc_ref) = refs[2 * P:]
    s = pl.program_id(1)
    rm = MOBA_HEADS * n_tok
    rd = DIFF_MAPS * n_tok
    lane = lax.broadcasted_iota(jnp.int32, (1, LANES), 1)

    @pl.when(s == 0)
    def _():
        q = q_ref[0]
        qm = jnp.concatenate([q[:, :MOBA_W]] * MOBA_HEADS, axis=0)
        r = lax.broadcasted_iota(jnp.int32, qm.shape, 0) // n_tok
        c = lax.broadcasted_iota(jnp.int32, qm.shape, 1) // HEAD_DIM
        qm_ref[...] = jnp.where(r == c, qm, 0.0).astype(BF16)
        qd = jnp.concatenate([q[:, MOBA_W:]] * DIFF_MAPS, axis=0)
        r = lax.broadcasted_iota(jnp.int32, qd.shape, 0) // n_tok
        c = lax.broadcasted_iota(jnp.int32, qd.shape, 1) // DIFF_QK
        qd_ref[...] = jnp.where(r == c, qd, 0.0).astype(BF16)
        gate_ref[...] = jnp.full(gate_ref.shape, -jnp.inf, F32)
        mb_ref[...] = jnp.full(mb_ref.shape, NEG, F32)
        lb_ref[...] = jnp.zeros(lb_ref.shape, F32)
        m_ref[...] = jnp.full(m_ref.shape, NEG, F32)
        l_ref[...] = jnp.zeros(l_ref.shape, F32)
        acc_ref[...] = jnp.zeros(acc_ref.shape, F32)

    def scores(q, k, transposed):
        return jnp.dot(q, k, preferred_element_type=F32) if transposed else _qk(q, k)

    def weighted(p, v, transposed):
        return _qk(p, v) if transposed else jnp.dot(p, v, preferred_element_type=F32)

    def diff_update(kd, vd, bias, transposed):
        sd = scores(qd_ref[...], kd, transposed) + bias
        m_old = m_ref[...]
        m = jnp.maximum(m_old, jnp.max(sd, axis=1, keepdims=True))
        alpha = jnp.exp(m_old - m)
        p = jnp.exp(sd - m)
        m_ref[...] = m
        l_ref[...] = alpha * l_ref[...] + jnp.sum(p, axis=1, keepdims=True)
        acc_ref[...] = alpha * acc_ref[...] + weighted(p.astype(BF16), vd, transposed)

    def moba_partial(km, vm, bias, transposed):
        raw = scores(qm_ref[...], km, transposed)
        sm = raw + bias
        m = jnp.max(sm, axis=1, keepdims=True)
        p = jnp.exp(sm - m)
        return raw, m, jnp.sum(p, axis=1, keepdims=True), weighted(p.astype(BF16), vm, transposed)

    page = lambda ref: ref[0, 0].reshape(QKV_W, PAGE_SIZE)
    kt = jnp.concatenate([page(r) for r in k_refs], axis=1).astype(BF16)
    vt = jnp.concatenate([page(r) for r in v_refs], axis=1).astype(BF16)
    blk = MOBA_BLOCK
    nb = n_steps * (P // 2)

    def bias_of(n, tail_ref, far_ref):
        t = n - (nb - n_tail)
        return jnp.where(t >= 0, tail_ref[jnp.maximum(t, 0)], far_ref[...])

    raw = jnp.dot(qm_ref[...], kt[:MOBA_W], preferred_element_type=F32)
    gate, mb, lb = gate_ref[...], mb_ref[...], lb_ref[...]
    for bi in range(P // 2):
        n = s * (P // 2) + bi
        cols = slice(bi * blk, (bi + 1) * blk)
        sm = raw[:, cols] + bias_of(n, bm_ref, bmf_ref)
        m = jnp.max(sm, axis=1, keepdims=True)
        p = jnp.exp(sm - m)
        accb_ref[n] = _qk(p.astype(BF16), vt[:MOBA_W, cols])
        gate = jnp.where(lane == n, jnp.mean(raw[:, cols], axis=1, keepdims=True), gate)
        mb = jnp.where(lane == n, m, mb)
        lb = jnp.where(lane == n, jnp.sum(p, axis=1, keepdims=True), lb)
    gate_ref[...], mb_ref[...], lb_ref[...] = gate, mb, lb
    bias_d = jnp.concatenate([bias_of(s * (P // 2) + bi, bd_ref, bdf_ref) for bi in range(P // 2)], axis=1)
    diff_update(kt[MOBA_W:], vt[MOBA_W:], bias_d, True)

    @pl.when(s == n_steps - 1)
    def _():
        kn = kn_ref[0].astype(BF16)
        vn = vn_ref[0].astype(BF16)
        diff_update(kn[:, MOBA_W:], vn[:, MOBA_W:], bdo_ref[...], False)
        _, m_own, l_own, acc_own = moba_partial(kn[:, :MOBA_W], vn[:, :MOBA_W], bmo_ref[...], False)
        col = lax.broadcasted_iota(jnp.int32, (rm, LANES), 1)
        sel = _top3_mask(gate_ref[...], col, 1)
        sel = jnp.where(gate_ref[...] > -jnp.inf, sel, 0.0)
        mb = mb_ref[...]
        m_fin = jnp.maximum(m_own, jnp.max(jnp.where(sel > 0.5, mb, NEG), axis=1, keepdims=True))
        w = sel * jnp.exp(jnp.minimum(mb - m_fin, 0.0))
        w_own = jnp.exp(m_own - m_fin)
        l_fin = w_own * l_own + jnp.sum(w * lb_ref[...], axis=1, keepdims=True)

        def merge(nb, acc):
            wn = jnp.sum(jnp.where(col == nb, w, 0.0), axis=1, keepdims=True)
            return acc + wn * accb_ref[nb]

        acc_fin = lax.fori_loop(0, n_steps * (P // 2), merge, w_own * acc_own)
        om_rows = acc_fin / l_fin
        lane_m = lax.broadcasted_iota(jnp.int32, (n_tok, MOBA_W), 1) // HEAD_DIM
        om = jnp.zeros((n_tok, MOBA_W), F32)
        for h in range(MOBA_HEADS):
            om = jnp.where(lane_m == h, om_rows[h * n_tok:(h + 1) * n_tok], om)

        od_rows = acc_ref[...] / l_ref[...]
        lane_d = lax.broadcasted_iota(jnp.int32, (n_tok, DIFF_W), 1) // HEAD_DIM
        lam = lam_ref[0, 0]
        od = jnp.zeros((n_tok, DIFF_W), F32)
        for h in range(DIFF_HEADS):
            a1 = od_rows[(2 * h) * n_tok:(2 * h + 1) * n_tok]
            a2 = od_rows[(2 * h + 1) * n_tok:(2 * h + 2) * n_tok]
            od = jnp.where(lane_d == h, a1 - lam * a2, od)
        normed = jnp.zeros_like(od)
        for h in range(DIFF_HEADS):
            ms = jnp.sum(jnp.where(lane_d == h, od * od, 0.0), axis=1, keepdims=True) * (1.0 / HEAD_DIM)
            normed = jnp.where(lane_d == h, od * lax.rsqrt(ms + RMS_EPS), normed)
        o_ref[0, :, :MOBA_W] = om
        o_ref[0, :, MOBA_W:] = normed * g_ref[...] * out_scale


def _sample_attn(layer, page_table, cache_kt, cache_vt, q, k_new, v_new,
                 bias_m, bias_d, bias_mf, bias_df, bias_mo, bias_do, lam, g, out_scale):
    nseq, n_pages = page_table.shape
    n_tok = q.shape[1]
    P = PAGES_PER_STEP if n_pages % PAGES_PER_STEP == 0 else PAGES_PER_STEP // 2
    assert n_pages % P == 0 and n_pages // 2 <= LANES and n_tok % 8 == 0
    n_steps = n_pages // P
    nb = n_pages // 2
    n_tail = bias_m.shape[0]
    rm, rd = MOBA_HEADS * n_tok, DIFF_MAPS * n_tok
    page = lambda j: pl.BlockSpec((1, 1, KV_HEADS, HEAD_DIM, PAGE_SIZE),
                                  lambda b, s, pt, j=j: (layer, pt[b, s * P + j], 0, 0, 0))
    const = lambda a: pl.BlockSpec(a.shape, lambda b, s, pt: (0,) * a.ndim)
    per_seq = lambda a: pl.BlockSpec((1,) + a.shape[1:], lambda b, s, pt: (b,) + (0,) * (a.ndim - 1))
    return pl.pallas_call(
        functools.partial(_sample_attn_kernel, n_tok=n_tok, n_steps=n_steps, n_tail=n_tail, P=P,
                          out_scale=out_scale),
        grid_spec=pltpu.PrefetchScalarGridSpec(
            num_scalar_prefetch=1,
            grid=(nseq, n_steps),
            in_specs=[pl.BlockSpec(memory_space=pltpu.SMEM)]
                     + [page(j) for j in range(P)] + [page(j) for j in range(P)]
                     + [per_seq(q), per_seq(k_new), per_seq(v_new),
                        const(bias_m), const(bias_d), const(bias_mf), const(bias_df), const(bias_mo), const(bias_do),
                        const(g)],
            out_specs=pl.BlockSpec((1, n_tok, QKV_W), lambda b, s, pt: (b, 0, 0)),
            scratch_shapes=[
                pltpu.VMEM((rm, MOBA_W), BF16), pltpu.VMEM((rd, DIFF_W), BF16),
                pltpu.VMEM((rm, LANES), F32), pltpu.VMEM((rm, LANES), F32), pltpu.VMEM((rm, LANES), F32),
                pltpu.VMEM((nb, rm, MOBA_W), F32),
                pltpu.VMEM((rd, 1), F32), pltpu.VMEM((rd, 1), F32), pltpu.VMEM((rd, DIFF_W), F32),
            ],
        ),
        out_shape=jax.ShapeDtypeStruct((nseq, n_tok, QKV_W), F32),
        compiler_params=_cparams(("parallel", "arbitrary")),
        name="sample_attn",
    )(page_table, lam, *([cache_kt] * P), *([cache_vt] * P), q, k_new, v_new,
      bias_m, bias_d, bias_mf, bias_df, bias_mo, bias_do, g)


def _sample_bias_kernel(rowtab_ref, o_ref, *, first_pos, past_len, n_tok):
    n = pl.program_id(0)
    rows, blk = o_ref.shape[1], o_ref.shape[2]
    t = lax.broadcasted_iota(jnp.int32, (rows, blk), 0) % n_tok
    pos = first_pos + n * blk + lax.broadcasted_iota(jnp.int32, (rows, blk), 1)
    o_ref[0] = _bias_of_distance(past_len + t - pos, lambda b: rowtab_ref[:, b:b + 1])


def _sample_bias(bias_table, past_len, n_tok):
    tab = bias_table.astype(F32).T
    tab_m = jnp.repeat(tab[:MOBA_HEADS], n_tok, axis=0)
    tab_d = jnp.repeat(tab[MOBA_HEADS:], 2 * n_tok, axis=0)
    nb = past_len // MOBA_BLOCK
    n_tail = min(nb, BIAS_TAIL_BLOCKS)
    assert n_tail == nb or n_tail * MOBA_BLOCK + 1 >= _bucket_thresholds()[-1]

    def past(rowtab):
        rows = rowtab.shape[0]
        return pl.pallas_call(
            functools.partial(_sample_bias_kernel, first_pos=(nb - n_tail) * MOBA_BLOCK, past_len=past_len,
                              n_tok=n_tok),
            grid=(n_tail,),
            in_specs=[pl.BlockSpec(rowtab.shape, lambda n: (0, 0))],
            out_specs=pl.BlockSpec((1, rows, MOBA_BLOCK), lambda n: (n, 0, 0)),
            out_shape=jax.ShapeDtypeStruct((n_tail, rows, MOBA_BLOCK), F32),
            compiler_params=_cparams(("parallel",)),
            name="sample_bias",
        )(rowtab)

    far = lambda rowtab: jnp.broadcast_to(rowtab[:, NUM_BUCKETS - 1:], (rowtab.shape[0], MOBA_BLOCK))

    t = np.arange(n_tok)
    d_own = t[:, None] - np.arange(LANES)[None, :]
    valid = (d_own >= 0) & (np.arange(LANES)[None, :] < n_tok)

    def own(rowtab):
        reps = rowtab.shape[0] // n_tok
        picked = jnp.take_along_axis(rowtab, jnp.asarray(np.tile(_bucket_np(d_own), (reps, 1))), axis=1)
        return jnp.where(jnp.asarray(np.tile(valid, (reps, 1))), picked, NEG)

    return past(tab_m), past(tab_d), far(tab_m), far(tab_d), own(tab_m), own(tab_d)


def _lambda_init(i):
    return 0.8 - 0.6 * math.exp(-0.3 * i)


def _perm_w_in(w):
    sizes = (MOBA_W, MOBA_W, MOBA_W, DIFF_W, DIFF_W, DIFF_W, GMLP_W, GMLP_W)
    o = np.concatenate([[0], np.cumsum(sizes)])
    qm, km, vm, qd, kd, vd, u, vg = [w[:, o[j]:o[j + 1]] for j in range(8)]
    return jnp.concatenate([qm, qd, km, kd, vm, vd, u, vg], axis=1).astype(BF16)


def _pad_router(w):
    wp = jnp.pad(w.astype(F32), ((0, 0), (0, LANES - N_EXPERTS)))
    return jnp.stack(_split3(wp))


TM_PROMPT, TM_SAMPLE = 512, 256
TMF_PROMPT, TMF_SAMPLE = 512, 128


def _trunk(h, p_layers, mix_fn, W, *, tm, tm_f):
    T = h.shape[0]
    depth = len(W)
    ks, vs, gs = [], [], []
    for i, lw in enumerate(W):
        om, od, og, k_out, v_out, vg = mix_fn(i, lw, h)
        h1, hn, route = _post_a(h, om, od, og, lw["w_out"], lw["g_ffn"], lw["w_router"], tm=tm)
        if lw["moe"]:
            te, nu, slot_token, slot_scale, slot = _moe_dispatch(route, tm=tm_f)
            y = _ffn(te, nu, jnp.take(hn, slot_token, axis=0), lw["wg"], lw["wu"], lw["wd"], slot_scale, tm=tm_f)
            ys = [jnp.take(y, slot[:, j], axis=0) for j in range(TOP_K)]
        else:
            n_tiles = T // tm_f
            ys = [_ffn(jnp.zeros((n_tiles,), jnp.int32), jnp.full((1,), n_tiles, jnp.int32), hn,
                       lw["wg"], lw["wu"], lw["wd"], jnp.ones((T, 1), F32), tm=tm_f)]
        h = _post_b(h1, ys, p_layers[i].reshape(T, D_PLE), lw["g_ple"], lw["w_ple_gate"], lw["w_ple_proj"],
                    lw["g_final"], tm=tm, final=(i == depth - 1))
        ks.append(k_out)
        vs.append(v_out)
        gs.append(vg)
    return h, ks, vs, gs


def kernel(x_prompt, x_sample, cache_k, cache_v, page_table, p_prompt, p_sample, bias_table, norm_mix, w_in, w_out, lambda_q1, lambda_k1, lambda_q2, lambda_k2, diff_norm, gmlp_w_s, gmlp_b_s, norm_ffn, w_ffn_gate, w_ffn_up, w_ffn_down, w_router, w_exp_gate, w_exp_up, w_exp_down, norm_ple, w_ple_gate, w_ple_proj, norm_final):
    B, S, _ = x_prompt.shape
    nseq, n_tok, _ = x_sample.shape
    depth = cache_k.shape[0]
    n_pages = page_table.shape[1]
    past_len = n_pages * PAGE_SIZE
    nblk = S // MOBA_BLOCK
    assert S % TM_PROMPT == 0 and nblk <= LANES and past_len % MOBA_BLOCK == 0
    assert GMLP_CHUNK % n_tok == 0 and (nseq * n_tok) % TM_SAMPLE == 0

    row2 = lambda a: a.astype(F32).reshape(1, -1)
    q_scale = jnp.asarray(np.concatenate([np.full(MOBA_W, HEAD_DIM ** -0.5, np.float32),
                                          np.full(DIFF_W, DIFF_QK ** -0.5, np.float32)])[None])
    tril = jnp.asarray(np.tril(np.ones((GMLP_CHUNK, GMLP_CHUNK), np.float32)))
    reps = GMLP_CHUNK // n_tok
    W = []
    for i in range(depth):
        ws = gmlp_w_s[i].astype(F32) * tril
        bs = gmlp_b_s[i].astype(F32)
        mix_prompt = (ws.astype(BF16), jnp.repeat(bs.T, HEAD_DIM, axis=1))
        w_small = ws[:, :n_tok, :n_tok]
        w_blockdiag = jnp.einsum("ab,gts->gatbs", jnp.eye(reps, dtype=F32), w_small).reshape(
            GMLP_GROUPS, GMLP_CHUNK, GMLP_CHUNK)
        mix_sample = (w_blockdiag.astype(BF16), jnp.tile(jnp.repeat(bs[:, :n_tok].T, HEAD_DIM, axis=1), (reps, 1)))
        lam = (jnp.exp(jnp.sum(lambda_q1[i].astype(F32) * lambda_k1[i].astype(F32)))
               - jnp.exp(jnp.sum(lambda_q2[i].astype(F32) * lambda_k2[i].astype(F32))) + _lambda_init(i))
        moe = i % 2 == 1
        j = i // 2
        w_perm = _perm_w_in(w_in[i])
        W.append(dict(
            g_mix=row2(norm_mix[i]), w_in=w_perm,
            w_in_t=w_perm[:, :3 * QKV_W].T.reshape(3, QKV_W, D_MODEL),
            w_in_kuv=jnp.concatenate([w_perm[:, QKV_W:2 * QKV_W], w_perm[:, 3 * QKV_W:]], axis=1),
            mix_prompt=mix_prompt, mix_sample=mix_sample,
            w_out=w_out[i].astype(BF16), g_ffn=row2(norm_ffn[i]),
            w_router=_pad_router(w_router[j]) if moe else None, moe=moe,
            wg=(w_exp_gate[j] if moe else w_ffn_gate[j][None]).astype(BF16),
            wu=(w_exp_up[j] if moe else w_ffn_up[j][None]).astype(BF16),
            wd=(w_exp_down[j] if moe else w_ffn_down[j][None]).astype(BF16),
            g_ple=row2(norm_ple[i]), w_ple_gate=w_ple_gate[i].astype(BF16), w_ple_proj=w_ple_proj[i].astype(BF16),
            g_final=row2(norm_final), lam=lam.reshape(1, 1).astype(F32),
            g_diff=diff_norm[i].astype(F32), out_scale=1.0 - _lambda_init(i),
        ))

    log2e = math.log2(math.e)
    bias_tiles = _prompt_bias_tiles(bias_table.astype(F32) * log2e).reshape(
        KV_HEADS // 2, 2, N_BIAS_TILES, MOBA_BLOCK, MOBA_BLOCK)
    q_scale_prompt = (q_scale * log2e).T
    nbp = -(-nblk // 16) * 16

    def prompt_mix(i, lw, h):
        qt, kb, kt, vt, vtb, og, km = _inproj_prompt(h, lw["g_mix"], lw["w_in_t"], lw["w_in_kuv"], q_scale_prompt,
                                                     *lw["mix_prompt"], B=B, S=S, tm=TM_PROMPT)
        k3 = kb.reshape(B, S, QKV_W)
        kmean = jnp.pad(km.reshape(B, nblk, QKV_W), ((0, 0), (0, nbp - nblk), (0, 0)))
        om = _moba_prompt(qt, k3, vtb, kmean, bias_tiles)
        od = _diff_prompt(qt, k3, vtb, bias_tiles, lw["lam"], lw["g_diff"].reshape(DIFF_HEADS // 2, LANES, 1),
                          lw["out_scale"])
        return om.reshape(B * S, MOBA_W), od.reshape(B * S, DIFF_W), og, kt, vt, None

    y_p, k_p, v_p, _ = _trunk(x_prompt.reshape(B * S, D_MODEL), p_prompt, prompt_mix, W,
                              tm=TM_PROMPT, tm_f=TMF_PROMPT)

    cache_kt = jnp.transpose(cache_k, (0, 1, 3, 4, 2))
    cache_vt = jnp.transpose(cache_v, (0, 1, 3, 4, 2))
    sbias = _sample_bias(bias_table, past_len, n_tok)

    def sample_mix(i, lw, h):
        q, kf, vf, og, vg = _inproj_sample(h, lw["g_mix"], lw["w_in"], q_scale, *lw["mix_sample"], tm=TM_SAMPLE)
        pad_new = lambda a: jnp.pad(a.reshape(nseq, n_tok, QKV_W), ((0, 0), (0, LANES - n_tok), (0, 0)))
        o = _sample_attn(i, page_table, cache_kt, cache_vt, q.reshape(nseq, n_tok, QKV_W),
                         pad_new(kf), pad_new(vf), *sbias, lw["lam"], lw["g_diff"].reshape(1, DIFF_W),
                         lw["out_scale"])
        o = o.reshape(nseq * n_tok, QKV_W).astype(BF16)
        return o[:, :MOBA_W], o[:, MOBA_W:], og, kf, vf, vg

    y_s, k_s, v_s, g_s = _trunk(x_sample.reshape(nseq * n_tok, D_MODEL), p_sample, sample_mix, W,
                                tm=TM_SAMPLE, tm_f=TMF_SAMPLE)

    heads_t = lambda cols: jnp.transpose(
        jnp.stack(cols).reshape(depth, B, KV_HEADS, HEAD_DIM, S), (0, 1, 4, 2, 3))
    heads = lambda rows: jnp.stack(rows).reshape(depth, nseq, n_tok, KV_HEADS, HEAD_DIM)
    return (y_p.reshape(B, S, D_MODEL), y_s.reshape(nseq, n_tok, D_MODEL),
            heads_t(k_p), heads_t(v_p), heads(k_s), heads(v_s),
            jnp.stack(g_s).reshape(depth, nseq, n_tok, GMLP_W))
```

```python
import functools
import math

import jax
import jax.numpy as jnp
import numpy as np
from jax import lax
from jax.experimental import pallas as pl
from jax.experimental.pallas import tpu as pltpu

F32 = jnp.float32
BF16 = jnp.bfloat16

D_MODEL = 1024
HEAD_DIM = 64
MOBA_HEADS = 6
DIFF_HEADS = 4
KV_HEADS = MOBA_HEADS + DIFF_HEADS
MOBA_W = MOBA_HEADS * HEAD_DIM
DIFF_W = DIFF_HEADS * HEAD_DIM
GMLP_GROUPS = 6
GMLP_W = GMLP_GROUPS * HEAD_DIM
QKV_W = MOBA_W + DIFF_W
DIFF_QK = HEAD_DIM // 2
MOBA_BLOCK = 256
MOBA_TOPK = 3
GMLP_CHUNK = 128
NUM_BUCKETS = 32
MAX_EXACT = 16
MAX_DISTANCE = 2048
D_FF = 2816
N_EXPERTS = 8
TOP_K = 2
D_PLE = 256
PAGE_SIZE = 128
RMS_EPS = 1e-6

LANES = 128
NEG = -1e30
FF_CHUNK = D_FF // 2
N_BIAS_TILES = 8
VMEM_LIMIT = 56 * 1024 * 1024


def _cparams(sem):
    return pltpu.CompilerParams(dimension_semantics=sem, vmem_limit_bytes=VMEM_LIMIT)


def _rms(x, g):
    return x * lax.rsqrt(jnp.mean(x * x, axis=-1, keepdims=True) + RMS_EPS) * g


def _bucket_np(dist):
    n = np.maximum(dist, 0)
    nf = np.maximum(n, MAX_EXACT).astype(np.float32)
    large = MAX_EXACT + (np.log(nf / np.float32(MAX_EXACT)) / np.float32(math.log(MAX_DISTANCE / MAX_EXACT))
                         * np.float32(NUM_BUCKETS - MAX_EXACT)).astype(np.int32)
    return np.where(n < MAX_EXACT, n, np.minimum(large, NUM_BUCKETS - 1)).astype(np.int32)


def _bucket_thresholds():
    buckets = _bucket_np(np.arange(2 * MAX_DISTANCE))
    assert np.all(np.diff(buckets) >= 0) and buckets[-1] == NUM_BUCKETS - 1
    return [int(np.argmax(buckets >= b)) for b in range(NUM_BUCKETS)]


def _bias_of_distance(dist, value_of_bucket):
    thr = _bucket_thresholds()
    out = jnp.zeros(dist.shape, F32) + value_of_bucket(0)
    for b in range(1, NUM_BUCKETS):
        out = jnp.where(dist >= thr[b], value_of_bucket(b), out)
    return out


def _prompt_bias_kernel(tab_ref, o_ref):
    h, d = pl.program_id(0), pl.program_id(1)
    blk = MOBA_BLOCK
    kj = lax.broadcasted_iota(jnp.int32, (blk, blk), 0)
    qi = lax.broadcasted_iota(jnp.int32, (blk, blk), 1)
    dist = qi - kj + blk * d
    bias = _bias_of_distance(dist, lambda b: tab_ref[h, b])
    o_ref[0, 0] = jnp.where(dist >= 0, bias, NEG)


def _prompt_bias_tiles(bias_table):
    assert MOBA_BLOCK * (N_BIAS_TILES - 2) + 1 >= _bucket_thresholds()[-1]
    blk = MOBA_BLOCK
    return pl.pallas_call(
        _prompt_bias_kernel,
        grid=(KV_HEADS, N_BIAS_TILES),
        in_specs=[pl.BlockSpec(memory_space=pltpu.SMEM)],
        out_specs=pl.BlockSpec((1, 1, blk, blk), lambda h, d: (h, d, 0, 0)),
        out_shape=jax.ShapeDtypeStruct((KV_HEADS, N_BIAS_TILES, blk, blk), F32),
        compiler_params=_cparams(("parallel", "parallel")),
        name="prompt_bias",
    )(bias_table.astype(F32).T)


def _gmlp_tile(u, vg, ws_ref, bs_ref, og_ref, tm):
    vgb = vg.astype(BF16)
    first = lax.broadcasted_iota(jnp.int32, (1, LANES), 1) < HEAD_DIM
    for c in range(tm // GMLP_CHUNK):
        rows = slice(c * GMLP_CHUNK, (c + 1) * GMLP_CHUNK)
        for p in range(GMLP_GROUPS // 2):
            cols = slice(p * LANES, (p + 1) * LANES)
            vc = vgb[rows, cols]
            m0 = jnp.dot(ws_ref[2 * p], vc, preferred_element_type=F32)
            m1 = jnp.dot(ws_ref[2 * p + 1], vc, preferred_element_type=F32)
            mixed = jnp.where(first, m0, m1) + bs_ref[:, cols]
            og_ref[rows, cols] = (u[rows, cols] * mixed).astype(BF16)


def _inproj_prompt_kernel(x_ref, g_ref, wt_ref, w_ref, qs_ref, ws_ref, bs_ref,
                          qt_ref, kb_ref, kt_ref, vt_ref, vtb_ref, og_ref, km_ref, *, tm):
    xn = _rms(x_ref[...], g_ref[...]).astype(BF16)
    proj_t = lambda wt: lax.dot_general(wt, xn, (((1,), (1,)), ((), ())), preferred_element_type=F32)
    qt_ref[0] = (proj_t(wt_ref[0]) * qs_ref[...]).astype(BF16)
    kt_ref[0] = proj_t(wt_ref[1])
    vt = proj_t(wt_ref[2])
    vt_ref[0] = vt
    k = jnp.dot(xn, w_ref[:, :QKV_W], preferred_element_type=F32)
    kb_ref[...] = k.astype(BF16)
    for j in range(tm // MOBA_BLOCK):
        cols = slice(j * MOBA_BLOCK, (j + 1) * MOBA_BLOCK)
        vtb_ref[0, j] = vt[:, cols].astype(BF16)
        km_ref[j] = jnp.mean(k[cols], axis=0, keepdims=True)
    u = jax.nn.gelu(jnp.dot(xn, w_ref[:, QKV_W:QKV_W + GMLP_W], preferred_element_type=F32))
    vg = jax.nn.gelu(jnp.dot(xn, w_ref[:, QKV_W + GMLP_W:], preferred_element_type=F32))
    _gmlp_tile(u, vg, ws_ref, bs_ref, og_ref, tm)


def _inproj_prompt(x, g, w_t, w_kuv, q_scale_col, w_mix, b_mix, *, B, S, tm):
    assert S % tm == 0 and tm % MOBA_BLOCK == 0
    nj = S // tm
    nkm = tm // MOBA_BLOCK
    row = lambda w: pl.BlockSpec((tm, w), lambda b, j: (b * nj + j, 0))
    full = lambda a: pl.BlockSpec(a.shape, lambda b, j: (0,) * a.ndim)
    tcol = pl.BlockSpec((1, QKV_W, tm), lambda b, j: (b, 0, j))
    out_shape = (
        jax.ShapeDtypeStruct((B, QKV_W, S), BF16),
        jax.ShapeDtypeStruct((B * S, QKV_W), BF16),
        jax.ShapeDtypeStruct((B, QKV_W, S), F32),
        jax.ShapeDtypeStruct((B, QKV_W, S), F32),
        jax.ShapeDtypeStruct((B, S // MOBA_BLOCK, QKV_W, MOBA_BLOCK), BF16),
        jax.ShapeDtypeStruct((B * S, GMLP_W), BF16),
        jax.ShapeDtypeStruct((B * S // MOBA_BLOCK, 1, QKV_W), F32),
    )
    out_specs = (tcol, row(QKV_W), tcol, tcol,
                 pl.BlockSpec((1, nkm, QKV_W, MOBA_BLOCK), lambda b, j: (b, j, 0, 0)),
                 row(GMLP_W),
                 pl.BlockSpec((nkm, 1, QKV_W), lambda b, j: (b * nj + j, 0, 0)))
    return pl.pallas_call(
        functools.partial(_inproj_prompt_kernel, tm=tm),
        grid=(B, nj),
        in_specs=[row(D_MODEL), full(g), full(w_t), full(w_kuv), full(q_scale_col), full(w_mix), full(b_mix)],
        out_specs=out_specs,
        out_shape=out_shape,
        compiler_params=_cparams(("parallel", "parallel")),
        name="inproj_prompt",
    )(x, g, w_t, w_kuv, q_scale_col, w_mix, b_mix)


def _inproj_sample_kernel(x_ref, g_ref, w_ref, qs_ref, ws_ref, bs_ref, q_ref, kf_ref, vf_ref, og_ref, vg_ref, *, tm):
    xn = _rms(x_ref[...], g_ref[...]).astype(BF16)
    c1, c2, c3, c4 = QKV_W, 2 * QKV_W, 3 * QKV_W, 3 * QKV_W + GMLP_W
    q_ref[...] = jnp.dot(xn, w_ref[:, :c1], preferred_element_type=F32) * qs_ref[...]
    kf_ref[...] = jnp.dot(xn, w_ref[:, c1:c2], preferred_element_type=F32)
    vf_ref[...] = jnp.dot(xn, w_ref[:, c2:c3], preferred_element_type=F32)
    u = jax.nn.gelu(jnp.dot(xn, w_ref[:, c3:c4], preferred_element_type=F32))
    vg = jax.nn.gelu(jnp.dot(xn, w_ref[:, c4:], preferred_element_type=F32))
    vg_ref[...] = vg
    _gmlp_tile(u, vg, ws_ref, bs_ref, og_ref, tm)


def _inproj_sample(x, g, w_perm, q_scale, w_mix, b_mix, *, tm):
    T = x.shape[0]
    assert T % tm == 0 and tm % GMLP_CHUNK == 0
    row = lambda w: pl.BlockSpec((tm, w), lambda i: (i, 0))
    full = lambda a: pl.BlockSpec(a.shape, lambda i: (0,) * a.ndim)
    out_shape = (
        jax.ShapeDtypeStruct((T, QKV_W), F32),
        jax.ShapeDtypeStruct((T, QKV_W), F32),
        jax.ShapeDtypeStruct((T, QKV_W), F32),
        jax.ShapeDtypeStruct((T, GMLP_W), BF16),
        jax.ShapeDtypeStruct((T, GMLP_W), F32),
    )
    return pl.pallas_call(
        functools.partial(_inproj_sample_kernel, tm=tm),
        grid=(T // tm,),
        in_specs=[row(D_MODEL), full(g), full(w_perm), full(q_scale), full(w_mix), full(b_mix)],
        out_specs=(row(QKV_W), row(QKV_W), row(QKV_W), row(GMLP_W), row(GMLP_W)),
        out_shape=out_shape,
        compiler_params=_cparams(("parallel",)),
        name="inproj_sample",
    )(x, g, w_perm, q_scale, w_mix, b_mix)


def _masked_queries(qt, group_width, n_maps):
    sub = lax.broadcasted_iota(jnp.int32, (LANES, 1), 0)
    zero = jnp.zeros_like(qt)
    return jnp.concatenate([jnp.where(sub // group_width == j, qt, zero) for j in range(n_maps)], axis=1)


def _softmax_probs(s, m_ref, l_ref, p_ref, alpha_ref):
    blk = MOBA_BLOCK
    for c in range(s.shape[1] // blk):
        cols = slice(c * blk, (c + 1) * blk)
        sc = s[:, cols]
        m_old = m_ref[:, cols]
        m = jnp.maximum(m_old, jnp.max(sc, axis=0, keepdims=True))
        alpha = jnp.exp2(m_old - m)
        p = jnp.exp2(sc - m)
        m_ref[:, cols] = m
        l_ref[:, cols] = alpha * l_ref[:, cols] + jnp.sum(p, axis=0, keepdims=True)
        alpha_ref[:, cols] = alpha
        p_ref[:, cols] = p.astype(BF16)


def _accumulate_values(vt, p_ref, alpha_ref, acc_ref):
    blk = MOBA_BLOCK
    n_maps = p_ref.shape[1] // blk
    for c in range(n_maps):
        cols = slice(c * blk, (c + 1) * blk)
        head = c * 2 // n_maps
        pv = jnp.dot(vt[head * HEAD_DIM:(head + 1) * HEAD_DIM], p_ref[:, cols], preferred_element_type=F32)
        acc_ref[:, cols] = alpha_ref[:, cols] * acc_ref[:, cols] + pv


def _attend_blocks(i, k_ref, vt_ref, qw_ref, s_refs, p_refs, alpha_refs, m_ref, l_ref, acc_ref, bias, mask,
                   defer_values):
    blk = MOBA_BLOCK

    def scores(n, dst):
        kb = k_ref[0, pl.ds(pl.multiple_of(jnp.minimum(n, i) * blk, blk), blk), :]
        dst[...] = jnp.dot(kb, qw_ref[...], preferred_element_type=F32)

    def stage(n, x):
        scores(n + 1, s_refs[1 - x])
        s = s_refs[x][...] + bias(jnp.minimum(i - n, N_BIAS_TILES - 1))
        _softmax_probs(mask(s, n), m_ref, l_ref, p_refs[x], alpha_refs[x])
        if defer_values:
            _accumulate_values(vt_ref[0, jnp.maximum(n - 1, 0)], p_refs[1 - x], alpha_refs[1 - x], acc_ref)
        else:
            _accumulate_values(vt_ref[0, n], p_refs[x], alpha_refs[x], acc_ref)

    m_ref[...] = jnp.full(m_ref.shape, NEG, F32)
    l_ref[...] = jnp.zeros(l_ref.shape, F32)
    acc_ref[...] = jnp.zeros(acc_ref.shape, F32)
    if defer_values:
        p_refs[1][...] = jnp.zeros(p_refs[1].shape, BF16)
        alpha_refs[1][...] = jnp.ones(alpha_refs[1].shape, F32)
    scores(0, s_refs[0])

    def body(j, carry):
        n = 2 * j
        stage(n, 0)

        @pl.when(n + 1 <= i)
        def _():
            stage(n + 1, 1)

        return carry

    lax.fori_loop(0, (i + 2) // 2, body, 0)
    for x in range(2 if defer_values else 0):
        @pl.when(i % 2 == x)
        def _():
            _accumulate_values(vt_ref[0, i], p_refs[x], alpha_refs[x], acc_ref)


def _qk(qm, k):
    return lax.dot_general(qm, k, (((1,), (1,)), ((), ())), preferred_element_type=F32)


def _split3(x):
    hi = x.astype(BF16)
    r = x - hi.astype(F32)
    mid = r.astype(BF16)
    lo = (r - mid.astype(F32)).astype(BF16)
    return hi, mid, lo


def _top3_mask(gate, pos, axis):
    sel = jnp.zeros(gate.shape, F32)
    for _ in range(MOBA_TOPK):
        m = jnp.max(gate, axis=axis, keepdims=True)
        idx = jnp.min(jnp.where(gate == m, pos, gate.shape[axis]), axis=axis, keepdims=True)
        hit = pos == idx
        sel = jnp.where(hit, 1.0, sel)
        gate = jnp.where(hit, -jnp.inf, gate)
    return sel


def _moba_prompt_kernel(qt_ref, k_ref, vt_ref, km_ref, bias_ref, o_ref,
                        qw_ref, s0_ref, s1_ref, p0_ref, p1_ref, a0_ref, a1_ref, m_ref, l_ref, acc_ref, sel_ref):
    i = pl.program_id(2)
    blk = MOBA_BLOCK
    dot = lambda a, b: jnp.dot(a, b, preferred_element_type=F32)
    qw_ref[...] = _masked_queries(qt_ref[0], HEAD_DIM, 2)
    qw = qw_ref[...]
    blk_id = lax.broadcasted_iota(jnp.int32, sel_ref.shape, 0)
    km_hi, km_mid, km_lo = _split3(km_ref[0])
    gate = dot(km_hi, qw) + dot(km_mid, qw) + dot(km_lo, qw)
    gate = jnp.where(blk_id < i, gate, -jnp.inf)
    sel_ref[...] = jnp.where(blk_id < i, _top3_mask(gate, blk_id, 0), jnp.where(blk_id == i, 1.0, 0.0))
    bias = lambda d: jnp.concatenate([bias_ref[0, 0, d], bias_ref[0, 1, d]], axis=1)
    mask = lambda s, n: jnp.where(sel_ref[pl.ds(n, 1), :] > 0.5, s, NEG)
    _attend_blocks(i, k_ref, vt_ref, qw_ref, (s0_ref, s1_ref), (p0_ref, p1_ref), (a0_ref, a1_ref),
                   m_ref, l_ref, acc_ref, bias, mask, defer_values=True)
    o = acc_ref[...] / l_ref[...]
    o_ref[0] = jnp.concatenate([o[:, :blk], o[:, blk:]], axis=0).T.astype(BF16)


def _prompt_attn_specs(S, nbp, off):
    blk = MOBA_BLOCK
    return dict(
        qt=pl.BlockSpec((1, LANES, blk), lambda b, p, i: (b, p + off, i)),
        k=pl.BlockSpec((1, S, LANES), lambda b, p, i: (b, 0, p + off)),
        vt=pl.BlockSpec((1, S // blk, LANES, blk), lambda b, p, i: (b, 0, p + off, 0)),
        km=pl.BlockSpec((1, nbp, LANES), lambda b, p, i: (b, 0, p + off)),
        bias=pl.BlockSpec((1, 2, N_BIAS_TILES, blk, blk), lambda b, p, i: (p + off, 0, 0, 0, 0)),
        out=pl.BlockSpec((1, blk, LANES), lambda b, p, i: (b, i, p)),
    )


def _moba_prompt(qt, k, vtb, kmean, bias_tiles):
    B, S, _ = k.shape
    blk = MOBA_BLOCK
    nbp = kmean.shape[1]
    sp = _prompt_attn_specs(S, nbp, 0)
    return pl.pallas_call(
        _moba_prompt_kernel,
        grid=(B, MOBA_HEADS // 2, S // blk),
        in_specs=[sp["qt"], sp["k"], sp["vt"], sp["km"], sp["bias"]],
        out_specs=sp["out"],
        out_shape=jax.ShapeDtypeStruct((B, S, MOBA_W), BF16),
        scratch_shapes=[pltpu.VMEM((LANES, 2 * blk), BF16),
                        pltpu.VMEM((blk, 2 * blk), F32), pltpu.VMEM((blk, 2 * blk), F32),
                        pltpu.VMEM((blk, 2 * blk), BF16), pltpu.VMEM((blk, 2 * blk), BF16),
                        pltpu.VMEM((1, 2 * blk), F32), pltpu.VMEM((1, 2 * blk), F32),
                        pltpu.VMEM((1, 2 * blk), F32), pltpu.VMEM((1, 2 * blk), F32),
                        pltpu.VMEM((HEAD_DIM, 2 * blk), F32), pltpu.VMEM((nbp, 2 * blk), F32)],
        compiler_params=_cparams(("parallel", "parallel", "arbitrary")),
        name="moba_prompt",
    )(qt, k, vtb, kmean, bias_tiles)


def _diff_prompt_kernel(lam_ref, qt_ref, k_ref, vt_ref, bias_ref, g_ref, o_ref,
                        qw_ref, s0_ref, s1_ref, p0_ref, p1_ref, a0_ref, a1_ref, m_ref, l_ref, acc_ref, *, out_scale):
    i = pl.program_id(2)
    blk = MOBA_BLOCK
    qw_ref[...] = _masked_queries(qt_ref[0], DIFF_QK, 4)
    bias = lambda d: jnp.concatenate([bias_ref[0, j // 2, d] for j in range(4)], axis=1)
    _attend_blocks(i, k_ref, vt_ref, qw_ref, (s0_ref, s1_ref), (p0_ref, p1_ref), (a0_ref, a1_ref),
                   m_ref, l_ref, acc_ref, bias, lambda s, n: s, defer_values=False)
    lam = lam_ref[0, 0]
    o = acc_ref[...] / l_ref[...]
    outs = []
    for h in range(2):
        oh = o[:, (2 * h) * blk:(2 * h + 1) * blk] - lam * o[:, (2 * h + 1) * blk:(2 * h + 2) * blk]
        outs.append(oh * lax.rsqrt(jnp.mean(oh * oh, axis=0, keepdims=True) + RMS_EPS))
    ot = jnp.concatenate(outs, axis=0) * g_ref[0] * out_scale
    o_ref[0] = ot.T.astype(BF16)


def _diff_prompt(qt, k, vtb, bias_tiles, lam, g_col, out_scale):
    B, S, _ = k.shape
    blk = MOBA_BLOCK
    sp = _prompt_attn_specs(S, 0, MOBA_HEADS // 2)
    return pl.pallas_call(
        functools.partial(_diff_prompt_kernel, out_scale=out_scale),
        grid=(B, DIFF_HEADS // 2, S // blk),
        in_specs=[pl.BlockSpec(memory_space=pltpu.SMEM), sp["qt"], sp["k"], sp["vt"], sp["bias"],
                  pl.BlockSpec((1, LANES, 1), lambda b, p, i: (p, 0, 0))],
        out_specs=sp["out"],
        out_shape=jax.ShapeDtypeStruct((B, S, DIFF_W), BF16),
        scratch_shapes=[pltpu.VMEM((LANES, 4 * blk), BF16),
                        pltpu.VMEM((blk, 4 * blk), F32), pltpu.VMEM((blk, 4 * blk), F32),
                        pltpu.VMEM((blk, 4 * blk), BF16), pltpu.VMEM((blk, 4 * blk), BF16),
                        pltpu.VMEM((1, 4 * blk), F32), pltpu.VMEM((1, 4 * blk), F32),
                        pltpu.VMEM((1, 4 * blk), F32), pltpu.VMEM((1, 4 * blk), F32),
                        pltpu.VMEM((HEAD_DIM, 4 * blk), F32)],
        compiler_params=_cparams(("parallel", "parallel", "arbitrary")),
        name="diff_prompt",
    )(lam, qt, k, vtb, bias_tiles, g_col)


def _post_a_kernel(x_ref, om_ref, od_ref, og_ref, wo_ref, g_ref, *rest):
    h = x_ref[...]
    h = h + jnp.dot(om_ref[...], wo_ref[0:MOBA_W, :], preferred_element_type=F32)
    h = h + jnp.dot(od_ref[...], wo_ref[MOBA_W:QKV_W, :], preferred_element_type=F32)
    h = h + jnp.dot(og_ref[...], wo_ref[QKV_W:, :], preferred_element_type=F32)
    if len(rest) == 2:
        h_ref, hn_ref = rest
    else:
        wr_ref, h_ref, hn_ref, route_ref = rest
    h_ref[...] = h
    hn = _rms(h, g_ref[...])
    hn_ref[...] = hn.astype(BF16)
    if len(rest) == 2:
        return
    a_hi, a_mid, _ = _split3(hn)
    dot = lambda a, b: jnp.dot(a, b, preferred_element_type=F32)
    logits = dot(a_hi, wr_ref[0]) + (dot(a_hi, wr_ref[1]) + dot(a_mid, wr_ref[0]))
    col = lax.broadcasted_iota(jnp.int32, logits.shape, 1)
    logits = jnp.where(col < N_EXPERTS, logits, -jnp.inf)
    m1 = jnp.max(logits, axis=1, keepdims=True)
    i1 = jnp.min(jnp.where(logits == m1, col, LANES), axis=1, keepdims=True)
    rest = jnp.where(col == i1, -jnp.inf, logits)
    m2 = jnp.max(rest, axis=1, keepdims=True)
    i2 = jnp.min(jnp.where(rest == m2, col, LANES), axis=1, keepdims=True)
    e = jnp.exp(m2 - m1)
    w1 = 1.0 / (1.0 + e)
    w2 = e / (1.0 + e)
    route = jnp.where(col == 0, i1.astype(F32), 0.0)
    route = jnp.where(col == 1, i2.astype(F32), route)
    route = jnp.where(col == 2, w1, route)
    route = jnp.where(col == 3, w2, route)
    route_ref[...] = route


def _post_a(x, om, od, og, w_out, g, w_router3, *, tm):
    T = x.shape[0]
    row = lambda w: pl.BlockSpec((tm, w), lambda i: (i, 0))
    full = lambda a: pl.BlockSpec(a.shape, lambda i: (0,) * a.ndim)
    routed = w_router3 is not None
    outs = pl.pallas_call(
        _post_a_kernel,
        grid=(T // tm,),
        in_specs=[row(D_MODEL), row(MOBA_W), row(DIFF_W), row(GMLP_W), full(w_out), full(g)]
                 + ([full(w_router3)] if routed else []),
        out_specs=(row(D_MODEL), row(D_MODEL)) + ((row(LANES),) if routed else ()),
        out_shape=(jax.ShapeDtypeStruct((T, D_MODEL), F32), jax.ShapeDtypeStruct((T, D_MODEL), BF16))
                  + ((jax.ShapeDtypeStruct((T, LANES), F32),) if routed else ()),
        compiler_params=_cparams(("parallel",)),
        name="post_a",
    )(x, om, od, og, w_out, g, *([w_router3] if routed else []))
    return outs if routed else (*outs, None)


def _ffn_kernel(te_ref, nu_ref, x_ref, wg_ref, wu_ref, wd_ref, sc_ref, o_ref, acc_ref):
    t = pl.program_id(0)
    f = pl.program_id(1)

    @pl.when(t < nu_ref[0])
    def _():
        x = x_ref[...]
        g = jnp.dot(x, wg_ref[0], preferred_element_type=F32)
        u = jnp.dot(x, wu_ref[0], preferred_element_type=F32)
        a = (g * jax.nn.sigmoid(g) * u).astype(BF16)
        part = jnp.dot(a, wd_ref[0], preferred_element_type=F32)

        @pl.when(f == 0)
        def _():
            acc_ref[...] = part

        @pl.when(f == 1)
        def _():
            o_ref[...] = (acc_ref[...] + part) * sc_ref[...]

    @pl.when(jnp.logical_and(t >= nu_ref[0], f == 1))
    def _():
        o_ref[...] = jnp.zeros_like(o_ref)


def _ffn(tile_expert, n_used, x, wg, wu, wd, scale, *, tm):
    N = x.shape[0]
    n_tiles = N // tm
    chunk = lambda t, f: jnp.where(t % 2 == 0, f, 1 - f)
    return pl.pallas_call(
        _ffn_kernel,
        grid_spec=pltpu.PrefetchScalarGridSpec(
            num_scalar_prefetch=2,
            grid=(n_tiles, 2),
            in_specs=[
                pl.BlockSpec((tm, D_MODEL), lambda t, f, te, nu: (t, 0)),
                pl.BlockSpec((1, D_MODEL, FF_CHUNK), lambda t, f, te, nu: (te[t], 0, chunk(t, f))),
                pl.BlockSpec((1, D_MODEL, FF_CHUNK), lambda t, f, te, nu: (te[t], 0, chunk(t, f))),
                pl.BlockSpec((1, FF_CHUNK, D_MODEL), lambda t, f, te, nu: (te[t], chunk(t, f), 0)),
                pl.BlockSpec((tm, 1), lambda t, f, te, nu: (t, 0)),
            ],
            out_specs=pl.BlockSpec((tm, D_MODEL), lambda t, f, te, nu: (t, 0)),
            scratch_shapes=[pltpu.VMEM((tm, D_MODEL), F32)],
        ),
        out_shape=jax.ShapeDtypeStruct((N, D_MODEL), F32),
        compiler_params=_cparams(("arbitrary", "arbitrary")),
        name="ffn",
    )(tile_expert, n_used, x, wg, wu, wd, scale)


def _moe_dispatch(route, *, tm):
    T = route.shape[0]
    n_assign = T * TOP_K
    n_tiles = -(-n_assign // tm) + N_EXPERTS
    experts = route[:, :TOP_K].astype(jnp.int32).reshape(n_assign)
    weights = route[:, TOP_K:2 * TOP_K].reshape(n_assign)
    onehot = (experts[:, None] == jnp.arange(N_EXPERTS)[None, :]).astype(jnp.int32)
    rank = jnp.sum((jnp.cumsum(onehot, axis=0) - onehot) * onehot, axis=1)
    counts = jnp.sum(onehot, axis=0)
    tiles_per = (counts + tm - 1) // tm
    tile_end = jnp.cumsum(tiles_per)
    slot = ((tile_end - tiles_per) * tm)[experts] + rank
    tile_expert = jnp.minimum(jnp.searchsorted(tile_end, jnp.arange(n_tiles), side="right"),
                              N_EXPERTS - 1).astype(jnp.int32)
    n_used = tile_end[-1:].astype(jnp.int32)
    token = jnp.arange(n_assign, dtype=jnp.int32) // TOP_K
    slot_token = jnp.zeros((n_tiles * tm,), jnp.int32).at[slot].set(token)
    slot_scale = jnp.zeros((n_tiles * tm,), F32).at[slot].set(weights)
    return tile_expert, n_used, slot_token, slot_scale[:, None], slot.reshape(T, TOP_K)


def _post_b_kernel(*refs, n_y, final):
    h_ref = refs[0]
    y_refs = refs[1:1 + n_y]
    p_ref, gp_ref, wgate_ref, wproj_ref, gf_ref, o_ref = refs[1 + n_y:]
    h = h_ref[...]
    for y_ref in y_refs:
        h = h + y_ref[...]
    gate = jax.nn.sigmoid(jnp.dot(_rms(h, gp_ref[...]).astype(BF16), wgate_ref[...], preferred_element_type=F32))
    h = h + gate * jnp.dot(p_ref[...].astype(BF16), wproj_ref[...], preferred_element_type=F32)
    o_ref[...] = _rms(h, gf_ref[...]) if final else h


def _post_b(h, ys, p, g_ple, w_gate, w_proj, g_final, *, tm, final):
    T = h.shape[0]
    row = lambda w: pl.BlockSpec((tm, w), lambda i: (i, 0))
    full = lambda a: pl.BlockSpec(a.shape, lambda i: (0,) * a.ndim)
    return pl.pallas_call(
        functools.partial(_post_b_kernel, n_y=len(ys), final=final),
        grid=(T // tm,),
        in_specs=[row(D_MODEL)] + [row(D_MODEL)] * len(ys)
                 + [row(D_PLE), full(g_ple), full(w_gate), full(w_proj), full(g_final)],
        out_specs=row(D_MODEL),
        out_shape=jax.ShapeDtypeStruct((T, D_MODEL), F32),
        compiler_params=_cparams(("parallel",)),
        name="post_b",
    )(h, *ys, p, g_ple, w_gate, w_proj, g_final)


PAGES_PER_STEP = 16
BIAS_TAIL_BLOCKS = 8
DIFF_MAPS = 2 * DIFF_HEADS


def _sample_attn_kernel(pt_ref, lam_ref, *refs, n_tok, n_steps, n_tail, P, out_scale):
    k_refs = refs[:P]
    v_refs = refs[P:2 * P]
    (q_ref, kn_ref, vn_ref, bm_ref, bd_ref, bmf_ref, bdf_ref, bmo_ref, bdo_ref, g_ref, o_ref,
     qm_ref, qd_ref, gate_ref, mb_ref, lb_ref, accb_ref, m_ref, l_ref, acc_ref) = refs[2 * P:]
    s = pl.program_id(1)
    rm = MOBA_HEADS * n_tok
    rd = DIFF_MAPS * n_tok
    lane = lax.broadcasted_iota(jnp.int32, (1, LANES), 1)

    @pl.when(s == 0)
    def _():
        q = q_ref[0]
        qm = jnp.concatenate([q[:, :MOBA_W]] * MOBA_HEADS, axis=0)
        r = lax.broadcasted_iota(jnp.int32, qm.shape, 0) // n_tok
        c = lax.broadcasted_iota(jnp.int32, qm.shape, 1) // HEAD_DIM
        qm_ref[...] = jnp.where(r == c, qm, 0.0).astype(BF16)
        qd = jnp.concatenate([q[:, MOBA_W:]] * DIFF_MAPS, axis=0)
        r = lax.broadcasted_iota(jnp.int32, qd.shape, 0) // n_tok
        c = lax.broadcasted_iota(jnp.int32, qd.shape, 1) // DIFF_QK
        qd_ref[...] = jnp.where(r == c, qd, 0.0).astype(BF16)
        gate_ref[...] = jnp.full(gate_ref.shape, -jnp.inf, F32)
        mb_ref[...] = jnp.full(mb_ref.shape, NEG, F32)
        lb_ref[...] = jnp.zeros(lb_ref.shape, F32)
        m_ref[...] = jnp.full(m_ref.shape, NEG, F32)
        l_ref[...] = jnp.zeros(l_ref.shape, F32)
        acc_ref[...] = jnp.zeros(acc_ref.shape, F32)

    def scores(q, k, transposed):
        return jnp.dot(q, k, preferred_element_type=F32) if transposed else _qk(q, k)

    def weighted(p, v, transposed):
        return _qk(p, v) if transposed else jnp.dot(p, v, preferred_element_type=F32)

    def diff_update(kd, vd, bias, transposed):
        sd = scores(qd_ref[...], kd, transposed) + bias
        m_old = m_ref[...]
        m = jnp.maximum(m_old, jnp.max(sd, axis=1, keepdims=True))
        alpha = jnp.exp(m_old - m)
        p = jnp.exp(sd - m)
        m_ref[...] = m
        l_ref[...] = alpha * l_ref[...] + jnp.sum(p, axis=1, keepdims=True)
        acc_ref[...] = alpha * acc_ref[...] + weighted(p.astype(BF16), vd, transposed)

    def moba_partial(km, vm, bias, transposed):
        raw = scores(qm_ref[...], km, transposed)
        sm = raw + bias
        m = jnp.max(sm, axis=1, keepdims=True)
        p = jnp.exp(sm - m)
        return raw, m, jnp.sum(p, axis=1, keepdims=True), weighted(p.astype(BF16), vm, transposed)

    page = lambda ref: ref[0, 0].reshape(QKV_W, PAGE_SIZE)
    kt = jnp.concatenate([page(r) for r in k_refs], axis=1).astype(BF16)
    vt = jnp.concatenate([page(r) for r in v_refs], axis=1).astype(BF16)
    blk = MOBA_BLOCK
    nb = n_steps * (P // 2)

    def bias_of(n, tail_ref, far_ref):
        t = n - (nb - n_tail)
        return jnp.where(t >= 0, tail_ref[jnp.maximum(t, 0)], far_ref[...])

    raw = jnp.dot(qm_ref[...], kt[:MOBA_W], preferred_element_type=F32)
    gate, mb, lb = gate_ref[...], mb_ref[...], lb_ref[...]
    for bi in range(P // 2):
        n = s * (P // 2) + bi
        cols = slice(bi * blk, (bi + 1) * blk)
        sm = raw[:, cols] + bias_of(n, bm_ref, bmf_ref)
        m = jnp.max(sm, axis=1, keepdims=True)
        p = jnp.exp(sm - m)
        accb_ref[n] = _qk(p.astype(BF16), vt[:MOBA_W, cols])
        gate = jnp.where(lane == n, jnp.mean(raw[:, cols], axis=1, keepdims=True), gate)
        mb = jnp.where(lane == n, m, mb)
        lb = jnp.where(lane == n, jnp.sum(p, axis=1, keepdims=True), lb)
    gate_ref[...], mb_ref[...], lb_ref[...] = gate, mb, lb
    bias_d = jnp.concatenate([bias_of(s * (P // 2) + bi, bd_ref, bdf_ref) for bi in range(P // 2)], axis=1)
    diff_update(kt[MOBA_W:], vt[MOBA_W:], bias_d, True)

    @pl.when(s == n_steps - 1)
    def _():
        kn = kn_ref[0].astype(BF16)
        vn = vn_ref[0].astype(BF16)
        diff_update(kn[:, MOBA_W:], vn[:, MOBA_W:], bdo_ref[...], False)
        _, m_own, l_own, acc_own = moba_partial(kn[:, :MOBA_W], vn[:, :MOBA_W], bmo_ref[...], False)
        col = lax.broadcasted_iota(jnp.int32, (rm, LANES), 1)
        sel = _top3_mask(gate_ref[...], col, 1)
        sel = jnp.where(gate_ref[...] > -jnp.inf, sel, 0.0)
        mb = mb_ref[...]
        m_fin = jnp.maximum(m_own, jnp.max(jnp.where(sel > 0.5, mb, NEG), axis=1, keepdims=True))
        w = sel * jnp.exp(jnp.minimum(mb - m_fin, 0.0))
        w_own = jnp.exp(m_own - m_fin)
        l_fin = w_own * l_own + jnp.sum(w * lb_ref[...], axis=1, keepdims=True)

        def merge(nb, acc):
            wn = jnp.sum(jnp.where(col == nb, w, 0.0), axis=1, keepdims=True)
            return acc + wn * accb_ref[nb]

        acc_fin = lax.fori_loop(0, n_steps * (P // 2), merge, w_own * acc_own)
        om_rows = acc_fin / l_fin
        lane_m = lax.broadcasted_iota(jnp.int32, (n_tok, MOBA_W), 1) // HEAD_DIM
        om = jnp.zeros((n_tok, MOBA_W), F32)
        for h in range(MOBA_HEADS):
            om = jnp.where(lane_m == h, om_rows[h * n_tok:(h + 1) * n_tok], om)

        od_rows = acc_ref[...] / l_ref[...]
        lane_d = lax.broadcasted_iota(jnp.int32, (n_tok, DIFF_W), 1) // HEAD_DIM
        lam = lam_ref[0, 0]
        od = jnp.zeros((n_tok, DIFF_W), F32)
        for h in range(DIFF_HEADS):
            a1 = od_rows[(2 * h) * n_tok:(2 * h + 1) * n_tok]
            a2 = od_rows[(2 * h + 1) * n_tok:(2 * h + 2) * n_tok]
            od = jnp.where(lane_d == h, a1 - lam * a2, od)
        normed = jnp.zeros_like(od)
        for h in range(DIFF_HEADS):
            ms = jnp.sum(jnp.where(lane_d == h, od * od, 0.0), axis=1, keepdims=True) * (1.0 / HEAD_DIM)
            normed = jnp.where(lane_d == h, od * lax.rsqrt(ms + RMS_EPS), normed)
        o_ref[0, :, :MOBA_W] = om
        o_ref[0, :, MOBA_W:] = normed * g_ref[...] * out_scale


def _sample_attn(layer, page_table, cache_kt, cache_vt, q, k_new, v_new,
                 bias_m, bias_d, bias_mf, bias_df, bias_mo, bias_do, lam, g, out_scale):
    nseq, n_pages = page_table.shape
    n_tok = q.shape[1]
    P = PAGES_PER_STEP if n_pages % PAGES_PER_STEP == 0 else PAGES_PER_STEP // 2
    assert n_pages % P == 0 and n_pages // 2 <= LANES and n_tok % 8 == 0
    n_steps = n_pages // P
    nb = n_pages // 2
    n_tail = bias_m.shape[0]
    rm, rd = MOBA_HEADS * n_tok, DIFF_MAPS * n_tok
    page = lambda j: pl.BlockSpec((1, 1, KV_HEADS, HEAD_DIM, PAGE_SIZE),
                                  lambda b, s, pt, j=j: (layer, pt[b, s * P + j], 0, 0, 0))
    const = lambda a: pl.BlockSpec(a.shape, lambda b, s, pt: (0,) * a.ndim)
    per_seq = lambda a: pl.BlockSpec((1,) + a.shape[1:], lambda b, s, pt: (b,) + (0,) * (a.ndim - 1))
    return pl.pallas_call(
        functools.partial(_sample_attn_kernel, n_tok=n_tok, n_steps=n_steps, n_tail=n_tail, P=P,
                          out_scale=out_scale),
        grid_spec=pltpu.PrefetchScalarGridSpec(
            num_scalar_prefetch=1,
            grid=(nseq, n_steps),
            in_specs=[pl.BlockSpec(memory_space=pltpu.SMEM)]
                     + [page(j) for j in range(P)] + [page(j) for j in range(P)]
                     + [per_seq(q), per_seq(k_new), per_seq(v_new),
                        const(bias_m), const(bias_d), const(bias_mf), const(bias_df), const(bias_mo), const(bias_do),
                        const(g)],
            out_specs=pl.BlockSpec((1, n_tok, QKV_W), lambda b, s, pt: (b, 0, 0)),
            scratch_shapes=[
                pltpu.VMEM((rm, MOBA_W), BF16), pltpu.VMEM((rd, DIFF_W), BF16),
                pltpu.VMEM((rm, LANES), F32), pltpu.VMEM((rm, LANES), F32), pltpu.VMEM((rm, LANES), F32),
                pltpu.VMEM((nb, rm, MOBA_W), F32),
                pltpu.VMEM((rd, 1), F32), pltpu.VMEM((rd, 1), F32), pltpu.VMEM((rd, DIFF_W), F32),
            ],
        ),
        out_shape=jax.ShapeDtypeStruct((nseq, n_tok, QKV_W), F32),
        compiler_params=_cparams(("parallel", "arbitrary")),
        name="sample_attn",
    )(page_table, lam, *([cache_kt] * P), *([cache_vt] * P), q, k_new, v_new,
      bias_m, bias_d, bias_mf, bias_df, bias_mo, bias_do, g)


def _sample_bias_kernel(rowtab_ref, o_ref, *, first_pos, past_len, n_tok):
    n = pl.program_id(0)
    rows, blk = o_ref.shape[1], o_ref.shape[2]
    t = lax.broadcasted_iota(jnp.int32, (rows, blk), 0) % n_tok
    pos = first_pos + n * blk + lax.broadcasted_iota(jnp.int32, (rows, blk), 1)
    o_ref[0] = _bias_of_distance(past_len + t - pos, lambda b: rowtab_ref[:, b:b + 1])


def _sample_bias(bias_table, past_len, n_tok):
    tab = bias_table.astype(F32).T
    tab_m = jnp.repeat(tab[:MOBA_HEADS], n_tok, axis=0)
    tab_d = jnp.repeat(tab[MOBA_HEADS:], 2 * n_tok, axis=0)
    nb = past_len // MOBA_BLOCK
    n_tail = min(nb, BIAS_TAIL_BLOCKS)
    assert n_tail == nb or n_tail * MOBA_BLOCK + 1 >= _bucket_thresholds()[-1]

    def past(rowtab):
        rows = rowtab.shape[0]
        return pl.pallas_call(
            functools.partial(_sample_bias_kernel, first_pos=(nb - n_tail) * MOBA_BLOCK, past_len=past_len,
                              n_tok=n_tok),
            grid=(n_tail,),
            in_specs=[pl.BlockSpec(rowtab.shape, lambda n: (0, 0))],
            out_specs=pl.BlockSpec((1, rows, MOBA_BLOCK), lambda n: (n, 0, 0)),
            out_shape=jax.ShapeDtypeStruct((n_tail, rows, MOBA_BLOCK), F32),
            compiler_params=_cparams(("parallel",)),
            name="sample_bias",
        )(rowtab)

    far = lambda rowtab: jnp.broadcast_to(rowtab[:, NUM_BUCKETS - 1:], (rowtab.shape[0], MOBA_BLOCK))

    t = np.arange(n_tok)
    d_own = t[:, None] - np.arange(LANES)[None, :]
    valid = (d_own >= 0) & (np.arange(LANES)[None, :] < n_tok)

    def own(rowtab):
        reps = rowtab.shape[0] // n_tok
        picked = jnp.take_along_axis(rowtab, jnp.asarray(np.tile(_bucket_np(d_own), (reps, 1))), axis=1)
        return jnp.where(jnp.asarray(np.tile(valid, (reps, 1))), picked, NEG)

    return past(tab_m), past(tab_d), far(tab_m), far(tab_d), own(tab_m), own(tab_d)


def _lambda_init(i):
    return 0.8 - 0.6 * math.exp(-0.3 * i)


def _perm_w_in(w):
    sizes = (MOBA_W, MOBA_W, MOBA_W, DIFF_W, DIFF_W, DIFF_W, GMLP_W, GMLP_W)
    o = np.concatenate([[0], np.cumsum(sizes)])
    qm, km, vm, qd, kd, vd, u, vg = [w[:, o[j]:o[j + 1]] for j in range(8)]
    return jnp.concatenate([qm, qd, km, kd, vm, vd, u, vg], axis=1).astype(BF16)


def _pad_router(w):
    wp = jnp.pad(w.astype(F32), ((0, 0), (0, LANES - N_EXPERTS)))
    return jnp.stack(_split3(wp))


TM_PROMPT, TM_SAMPLE = 512, 256
TMF_PROMPT, TMF_SAMPLE = 512, 128


def _trunk(h, p_layers, mix_fn, W, *, tm, tm_f):
    T = h.shape[0]
    depth = len(W)
    ks, vs, gs = [], [], []
    for i, lw in enumerate(W):
        om, od, og, k_out, v_out, vg = mix_fn(i, lw, h)
        h1, hn, route = _post_a(h, om, od, og, lw["w_out"], lw["g_ffn"], lw["w_router"], tm=tm)
        if lw["moe"]:
            te, nu, slot_token, slot_scale, slot = _moe_dispatch(route, tm=tm_f)
            rows = lambda a, idx: a.at[idx].get(mode="promise_in_bounds")
            y = _ffn(te, nu, rows(hn, slot_token), lw["wg"], lw["wu"], lw["wd"], slot_scale, tm=tm_f)
            ys = [rows(y, slot[:, j]) for j in range(TOP_K)]
        else:
            n_tiles = T // tm_f
            ys = [_ffn(jnp.zeros((n_tiles,), jnp.int32), jnp.full((1,), n_tiles, jnp.int32), hn,
                       lw["wg"], lw["wu"], lw["wd"], jnp.ones((T, 1), F32), tm=tm_f)]
        h = _post_b(h1, ys, p_layers[i].reshape(T, D_PLE), lw["g_ple"], lw["w_ple_gate"], lw["w_ple_proj"],
                    lw["g_final"], tm=tm, final=(i == depth - 1))
        ks.append(k_out)
        vs.append(v_out)
        gs.append(vg)
    return h, ks, vs, gs


def kernel(x_prompt, x_sample, cache_k, cache_v, page_table, p_prompt, p_sample, bias_table, norm_mix, w_in, w_out, lambda_q1, lambda_k1, lambda_q2, lambda_k2, diff_norm, gmlp_w_s, gmlp_b_s, norm_ffn, w_ffn_gate, w_ffn_up, w_ffn_down, w_router, w_exp_gate, w_exp_up, w_exp_down, norm_ple, w_ple_gate, w_ple_proj, norm_final):
    B, S, _ = x_prompt.shape
    nseq, n_tok, _ = x_sample.shape
    depth = cache_k.shape[0]
    n_pages = page_table.shape[1]
    past_len = n_pages * PAGE_SIZE
    nblk = S // MOBA_BLOCK
    assert S % TM_PROMPT == 0 and nblk <= LANES and past_len % MOBA_BLOCK == 0
    assert GMLP_CHUNK % n_tok == 0 and (nseq * n_tok) % TM_SAMPLE == 0

    row2 = lambda a: a.astype(F32).reshape(1, -1)
    q_scale = jnp.asarray(np.concatenate([np.full(MOBA_W, HEAD_DIM ** -0.5, np.float32),
                                          np.full(DIFF_W, DIFF_QK ** -0.5, np.float32)])[None])
    tril = jnp.asarray(np.tril(np.ones((GMLP_CHUNK, GMLP_CHUNK), np.float32)))
    reps = GMLP_CHUNK // n_tok
    W = []
    for i in range(depth):
        ws = gmlp_w_s[i].astype(F32) * tril
        bs = gmlp_b_s[i].astype(F32)
        mix_prompt = (ws.astype(BF16), jnp.repeat(bs.T, HEAD_DIM, axis=1))
        w_small = ws[:, :n_tok, :n_tok]
        w_blockdiag = jnp.einsum("ab,gts->gatbs", jnp.eye(reps, dtype=F32), w_small).reshape(
            GMLP_GROUPS, GMLP_CHUNK, GMLP_CHUNK)
        mix_sample = (w_blockdiag.astype(BF16), jnp.tile(jnp.repeat(bs[:, :n_tok].T, HEAD_DIM, axis=1), (reps, 1)))
        lam = (jnp.exp(jnp.sum(lambda_q1[i].astype(F32) * lambda_k1[i].astype(F32)))
               - jnp.exp(jnp.sum(lambda_q2[i].astype(F32) * lambda_k2[i].astype(F32))) + _lambda_init(i))
        moe = i % 2 == 1
        j = i // 2
        w_perm = _perm_w_in(w_in[i])
        W.append(dict(
            g_mix=row2(norm_mix[i]), w_in=w_perm,
            w_in_t=w_perm[:, :3 * QKV_W].T.reshape(3, QKV_W, D_MODEL),
            w_in_kuv=jnp.concatenate([w_perm[:, QKV_W:2 * QKV_W], w_perm[:, 3 * QKV_W:]], axis=1),
            mix_prompt=mix_prompt, mix_sample=mix_sample,
            w_out=w_out[i].astype(BF16), g_ffn=row2(norm_ffn[i]),
            w_router=_pad_router(w_router[j]) if moe else None, moe=moe,
            wg=(w_exp_gate[j] if moe else w_ffn_gate[j][None]).astype(BF16),
            wu=(w_exp_up[j] if moe else w_ffn_up[j][None]).astype(BF16),
            wd=(w_exp_down[j] if moe else w_ffn_down[j][None]).astype(BF16),
            g_ple=row2(norm_ple[i]), w_ple_gate=w_ple_gate[i].astype(BF16), w_ple_proj=w_ple_proj[i].astype(BF16),
            g_final=row2(norm_final), lam=lam.reshape(1, 1).astype(F32),
            g_diff=diff_norm[i].astype(F32), out_scale=1.0 - _lambda_init(i),
        ))

    log2e = math.log2(math.e)
    bias_tiles = _prompt_bias_tiles(bias_table.astype(F32) * log2e).reshape(
        KV_HEADS // 2, 2, N_BIAS_TILES, MOBA_BLOCK, MOBA_BLOCK)
    q_scale_prompt = (q_scale * log2e).T
    nbp = -(-nblk // 16) * 16

    def prompt_mix(i, lw, h):
        qt, kb, kt, vt, vtb, og, km = _inproj_prompt(h, lw["g_mix"], lw["w_in_t"], lw["w_in_kuv"], q_scale_prompt,
                                                     *lw["mix_prompt"], B=B, S=S, tm=TM_PROMPT)
        k3 = kb.reshape(B, S, QKV_W)
        kmean = jnp.pad(km.reshape(B, nblk, QKV_W), ((0, 0), (0, nbp - nblk), (0, 0)))
        om = _moba_prompt(qt, k3, vtb, kmean, bias_tiles)
        od = _diff_prompt(qt, k3, vtb, bias_tiles, lw["lam"], lw["g_diff"].reshape(DIFF_HEADS // 2, LANES, 1),
                          lw["out_scale"])
        return om.reshape(B * S, MOBA_W), od.reshape(B * S, DIFF_W), og, kt, vt, None

    y_p, k_p, v_p, _ = _trunk(x_prompt.reshape(B * S, D_MODEL), p_prompt, prompt_mix, W,
                              tm=TM_PROMPT, tm_f=TMF_PROMPT)

    cache_kt = jnp.transpose(cache_k, (0, 1, 3, 4, 2))
    cache_vt = jnp.transpose(cache_v, (0, 1, 3, 4, 2))
    sbias = _sample_bias(bias_table, past_len, n_tok)

    def sample_mix(i, lw, h):
        q, kf, vf, og, vg = _inproj_sample(h, lw["g_mix"], lw["w_in"], q_scale, *lw["mix_sample"], tm=TM_SAMPLE)
        pad_new = lambda a: jnp.pad(a.reshape(nseq, n_tok, QKV_W), ((0, 0), (0, LANES - n_tok), (0, 0)))
        o = _sample_attn(i, page_table, cache_kt, cache_vt, q.reshape(nseq, n_tok, QKV_W),
                         pad_new(kf), pad_new(vf), *sbias, lw["lam"], lw["g_diff"].reshape(1, DIFF_W),
                         lw["out_scale"])
        o = o.reshape(nseq * n_tok, QKV_W).astype(BF16)
        return o[:, :MOBA_W], o[:, MOBA_W:], og, kf, vf, vg

    y_s, k_s, v_s, g_s = _trunk(x_sample.reshape(nseq * n_tok, D_MODEL), p_sample, sample_mix, W,
                                tm=TM_SAMPLE, tm_f=TMF_SAMPLE)

    heads_t = lambda cols: jnp.transpose(
        jnp.stack(cols).reshape(depth, B, KV_HEADS, HEAD_DIM, S), (0, 1, 4, 2, 3))
    heads = lambda rows: jnp.stack(rows).reshape(depth, nseq, n_tok, KV_HEADS, HEAD_DIM)
    return (y_p.reshape(B, S, D_MODEL), y_s.reshape(nseq, n_tok, D_MODEL),
            heads_t(k_p), heads_t(v_p), heads(k_s), heads(v_s),
            jnp.stack(g_s).reshape(depth, nseq, n_tok, GMLP_W))
```

```python
import functools
import math

import jax
import jax.numpy as jnp
import numpy as np
from jax import lax
from jax.experimental import pallas as pl
from jax.experimental.pallas import tpu as pltpu

F32 = jnp.float32
BF16 = jnp.bfloat16

D_MODEL = 1024
HEAD_DIM = 64
MOBA_HEADS = 6
DIFF_HEADS = 4
KV_HEADS = MOBA_HEADS + DIFF_HEADS
MOBA_W = MOBA_HEADS * HEAD_DIM
DIFF_W = DIFF_HEADS * HEAD_DIM
GMLP_GROUPS = 6
GMLP_W = GMLP_GROUPS * HEAD_DIM
QKV_W = MOBA_W + DIFF_W
DIFF_QK = HEAD_DIM // 2
MOBA_BLOCK = 256
MOBA_TOPK = 3
GMLP_CHUNK = 128
NUM_BUCKETS = 32
MAX_EXACT = 16
MAX_DISTANCE = 2048
D_FF = 2816
N_EXPERTS = 8
TOP_K = 2
D_PLE = 256
PAGE_SIZE = 128
RMS_EPS = 1e-6

LANES = 128
NEG = -1e30
FF_CHUNK = D_FF // 2
N_BIAS_TILES = 8
VMEM_LIMIT = 56 * 1024 * 1024


def _cparams(sem):
    return pltpu.CompilerParams(dimension_semantics=sem, vmem_limit_bytes=VMEM_LIMIT)


def _rms(x, g):
    return x * lax.rsqrt(jnp.mean(x * x, axis=-1, keepdims=True) + RMS_EPS) * g


def _bucket_np(dist):
    n = np.maximum(dist, 0)
    nf = np.maximum(n, MAX_EXACT).astype(np.float32)
    large = MAX_EXACT + (np.log(nf / np.float32(MAX_EXACT)) / np.float32(math.log(MAX_DISTANCE / MAX_EXACT))
                         * np.float32(NUM_BUCKETS - MAX_EXACT)).astype(np.int32)
    return np.where(n < MAX_EXACT, n, np.minimum(large, NUM_BUCKETS - 1)).astype(np.int32)


def _bucket_thresholds():
    buckets = _bucket_np(np.arange(2 * MAX_DISTANCE))
    assert np.all(np.diff(buckets) >= 0) and buckets[-1] == NUM_BUCKETS - 1
    return [int(np.argmax(buckets >= b)) for b in range(NUM_BUCKETS)]


def _bias_of_distance(dist, value_of_bucket):
    thr = _bucket_thresholds()
    out = jnp.zeros(dist.shape, F32) + value_of_bucket(0)
    for b in range(1, NUM_BUCKETS):
        out = jnp.where(dist >= thr[b], value_of_bucket(b), out)
    return out


def _prompt_bias_kernel(tab_ref, o_ref):
    h, d = pl.program_id(0), pl.program_id(1)
    blk = MOBA_BLOCK
    kj = lax.broadcasted_iota(jnp.int32, (blk, blk), 0)
    qi = lax.broadcasted_iota(jnp.int32, (blk, blk), 1)
    dist = qi - kj + blk * d
    bias = _bias_of_distance(dist, lambda b: tab_ref[h, b])
    o_ref[0, 0] = jnp.where(dist >= 0, bias, NEG)


def _prompt_bias_tiles(bias_table):
    assert MOBA_BLOCK * (N_BIAS_TILES - 2) + 1 >= _bucket_thresholds()[-1]
    blk = MOBA_BLOCK
    return pl.pallas_call(
        _prompt_bias_kernel,
        grid=(KV_HEADS, N_BIAS_TILES),
        in_specs=[pl.BlockSpec(memory_space=pltpu.SMEM)],
        out_specs=pl.BlockSpec((1, 1, blk, blk), lambda h, d: (h, d, 0, 0)),
        out_shape=jax.ShapeDtypeStruct((KV_HEADS, N_BIAS_TILES, blk, blk), F32),
        compiler_params=_cparams(("parallel", "parallel")),
        name="prompt_bias",
    )(bias_table.astype(F32).T)


def _gmlp_tile(u, vg, ws_ref, bs_ref, og_ref, tm):
    vgb = vg.astype(BF16)
    first = lax.broadcasted_iota(jnp.int32, (1, LANES), 1) < HEAD_DIM
    for c in range(tm // GMLP_CHUNK):
        rows = slice(c * GMLP_CHUNK, (c + 1) * GMLP_CHUNK)
        for p in range(GMLP_GROUPS // 2):
            cols = slice(p * LANES, (p + 1) * LANES)
            vc = vgb[rows, cols]
            m0 = jnp.dot(ws_ref[2 * p], vc, preferred_element_type=F32)
            m1 = jnp.dot(ws_ref[2 * p + 1], vc, preferred_element_type=F32)
            mixed = jnp.where(first, m0, m1) + bs_ref[:, cols]
            og_ref[rows, cols] = (u[rows, cols] * mixed).astype(BF16)


def _inproj_prompt_kernel(x_ref, g_ref, wt_ref, w_ref, qs_ref, ws_ref, bs_ref,
                          qt_ref, kb_ref, kt_ref, vt_ref, vtb_ref, og_ref, km_ref, *, tm):
    xn = _rms(x_ref[...], g_ref[...]).astype(BF16)
    proj_t = lambda wt: lax.dot_general(wt, xn, (((1,), (1,)), ((), ())), preferred_element_type=F32)
    qt_ref[0] = (proj_t(wt_ref[0]) * qs_ref[...]).astype(BF16)
    kt_ref[0] = proj_t(wt_ref[1])
    vt = proj_t(wt_ref[2])
    vt_ref[0] = vt
    k = jnp.dot(xn, w_ref[:, :QKV_W], preferred_element_type=F32)
    kb_ref[...] = k.astype(BF16)
    for j in range(tm // MOBA_BLOCK):
        cols = slice(j * MOBA_BLOCK, (j + 1) * MOBA_BLOCK)
        vtb_ref[0, j] = vt[:, cols].astype(BF16)
        km_ref[j] = jnp.mean(k[cols], axis=0, keepdims=True)
    u = jax.nn.gelu(jnp.dot(xn, w_ref[:, QKV_W:QKV_W + GMLP_W], preferred_element_type=F32))
    vg = jax.nn.gelu(jnp.dot(xn, w_ref[:, QKV_W + GMLP_W:], preferred_element_type=F32))
    _gmlp_tile(u, vg, ws_ref, bs_ref, og_ref, tm)


def _inproj_prompt(x, g, w_t, w_kuv, q_scale_col, w_mix, b_mix, *, B, S, tm):
    assert S % tm == 0 and tm % MOBA_BLOCK == 0
    nj = S // tm
    nkm = tm // MOBA_BLOCK
    row = lambda w: pl.BlockSpec((tm, w), lambda b, j: (b * nj + j, 0))
    full = lambda a: pl.BlockSpec(a.shape, lambda b, j: (0,) * a.ndim)
    tcol = pl.BlockSpec((1, QKV_W, tm), lambda b, j: (b, 0, j))
    out_shape = (
        jax.ShapeDtypeStruct((B, QKV_W, S), BF16),
        jax.ShapeDtypeStruct((B * S, QKV_W), BF16),
        jax.ShapeDtypeStruct((B, QKV_W, S), F32),
        jax.ShapeDtypeStruct((B, QKV_W, S), F32),
        jax.ShapeDtypeStruct((B, S // MOBA_BLOCK, QKV_W, MOBA_BLOCK), BF16),
        jax.ShapeDtypeStruct((B * S, GMLP_W), BF16),
        jax.ShapeDtypeStruct((B * S // MOBA_BLOCK, 1, QKV_W), F32),
    )
    out_specs = (tcol, row(QKV_W), tcol, tcol,
                 pl.BlockSpec((1, nkm, QKV_W, MOBA_BLOCK), lambda b, j: (b, j, 0, 0)),
                 row(GMLP_W),
                 pl.BlockSpec((nkm, 1, QKV_W), lambda b, j: (b * nj + j, 0, 0)))
    return pl.pallas_call(
        functools.partial(_inproj_prompt_kernel, tm=tm),
        grid=(B, nj),
        in_specs=[row(D_MODEL), full(g), full(w_t), full(w_kuv), full(q_scale_col), full(w_mix), full(b_mix)],
        out_specs=out_specs,
        out_shape=out_shape,
        compiler_params=_cparams(("parallel", "parallel")),
        name="inproj_prompt",
    )(x, g, w_t, w_kuv, q_scale_col, w_mix, b_mix)


def _inproj_sample_kernel(x_ref, g_ref, w_ref, qs_ref, ws_ref, bs_ref, q_ref, kf_ref, vf_ref, og_ref, vg_ref, *, tm):
    xn = _rms(x_ref[...], g_ref[...]).astype(BF16)
    c1, c2, c3, c4 = QKV_W, 2 * QKV_W, 3 * QKV_W, 3 * QKV_W + GMLP_W
    q_ref[...] = jnp.dot(xn, w_ref[:, :c1], preferred_element_type=F32) * qs_ref[...]
    kf_ref[...] = jnp.dot(xn, w_ref[:, c1:c2], preferred_element_type=F32)
    vf_ref[...] = jnp.dot(xn, w_ref[:, c2:c3], preferred_element_type=F32)
    u = jax.nn.gelu(jnp.dot(xn, w_ref[:, c3:c4], preferred_element_type=F32))
    vg = jax.nn.gelu(jnp.dot(xn, w_ref[:, c4:], preferred_element_type=F32))
    vg_ref[...] = vg
    _gmlp_tile(u, vg, ws_ref, bs_ref, og_ref, tm)


def _inproj_sample(x, g, w_perm, q_scale, w_mix, b_mix, *, tm):
    T = x.shape[0]
    assert T % tm == 0 and tm % GMLP_CHUNK == 0
    row = lambda w: pl.BlockSpec((tm, w), lambda i: (i, 0))
    full = lambda a: pl.BlockSpec(a.shape, lambda i: (0,) * a.ndim)
    out_shape = (
        jax.ShapeDtypeStruct((T, QKV_W), F32),
        jax.ShapeDtypeStruct((T, QKV_W), F32),
        jax.ShapeDtypeStruct((T, QKV_W), F32),
        jax.ShapeDtypeStruct((T, GMLP_W), BF16),
        jax.ShapeDtypeStruct((T, GMLP_W), F32),
    )
    return pl.pallas_call(
        functools.partial(_inproj_sample_kernel, tm=tm),
        grid=(T // tm,),
        in_specs=[row(D_MODEL), full(g), full(w_perm), full(q_scale), full(w_mix), full(b_mix)],
        out_specs=(row(QKV_W), row(QKV_W), row(QKV_W), row(GMLP_W), row(GMLP_W)),
        out_shape=out_shape,
        compiler_params=_cparams(("parallel",)),
        name="inproj_sample",
    )(x, g, w_perm, q_scale, w_mix, b_mix)


def _masked_queries(qt, group_width, n_maps):
    sub = lax.broadcasted_iota(jnp.int32, (LANES, 1), 0)
    zero = jnp.zeros_like(qt)
    return jnp.concatenate([jnp.where(sub // group_width == j, qt, zero) for j in range(n_maps)], axis=1)


def _softmax_probs(s, m_ref, l_ref, p_ref, alpha_ref):
    blk = MOBA_BLOCK
    for c in range(s.shape[1] // blk):
        cols = slice(c * blk, (c + 1) * blk)
        sc = s[:, cols]
        m_old = m_ref[:, cols]
        m = jnp.maximum(m_old, jnp.max(sc, axis=0, keepdims=True))
        alpha = jnp.exp2(m_old - m)
        p = jnp.exp2(sc - m)
        m_ref[:, cols] = m
        l_ref[:, cols] = alpha * l_ref[:, cols] + jnp.sum(p, axis=0, keepdims=True)
        alpha_ref[:, cols] = alpha
        p_ref[:, cols] = p.astype(BF16)


def _accumulate_values(vt, p_ref, alpha_ref, acc_ref):
    blk = MOBA_BLOCK
    n_maps = p_ref.shape[1] // blk
    for c in range(n_maps):
        cols = slice(c * blk, (c + 1) * blk)
        head = c * 2 // n_maps
        pv = jnp.dot(vt[head * HEAD_DIM:(head + 1) * HEAD_DIM], p_ref[:, cols], preferred_element_type=F32)
        acc_ref[:, cols] = alpha_ref[:, cols] * acc_ref[:, cols] + pv


def _attend_blocks(i, k_ref, vt_ref, qw_ref, s_refs, p_refs, alpha_refs, m_ref, l_ref, acc_ref, bias, mask,
                   defer_values):
    blk = MOBA_BLOCK

    def scores(n, dst):
        kb = k_ref[0, pl.ds(pl.multiple_of(jnp.minimum(n, i) * blk, blk), blk), :]
        dst[...] = jnp.dot(kb, qw_ref[...], preferred_element_type=F32)

    def stage(n, x):
        scores(n + 1, s_refs[1 - x])
        s = s_refs[x][...] + bias(jnp.minimum(i - n, N_BIAS_TILES - 1))
        _softmax_probs(mask(s, n), m_ref, l_ref, p_refs[x], alpha_refs[x])
        if defer_values:
            _accumulate_values(vt_ref[0, jnp.maximum(n - 1, 0)], p_refs[1 - x], alpha_refs[1 - x], acc_ref)
        else:
            _accumulate_values(vt_ref[0, n], p_refs[x], alpha_refs[x], acc_ref)

    m_ref[...] = jnp.full(m_ref.shape, NEG, F32)
    l_ref[...] = jnp.zeros(l_ref.shape, F32)
    acc_ref[...] = jnp.zeros(acc_ref.shape, F32)
    if defer_values:
        p_refs[1][...] = jnp.zeros(p_refs[1].shape, BF16)
        alpha_refs[1][...] = jnp.ones(alpha_refs[1].shape, F32)
    scores(0, s_refs[0])

    def body(j, carry):
        n = 2 * j
        stage(n, 0)

        @pl.when(n + 1 <= i)
        def _():
            stage(n + 1, 1)

        return carry

    lax.fori_loop(0, (i + 2) // 2, body, 0)
    for x in range(2 if defer_values else 0):
        @pl.when(i % 2 == x)
        def _():
            _accumulate_values(vt_ref[0, i], p_refs[x], alpha_refs[x], acc_ref)


def _qk(qm, k):
    return lax.dot_general(qm, k, (((1,), (1,)), ((), ())), preferred_element_type=F32)


def _split3(x):
    hi = x.astype(BF16)
    r = x - hi.astype(F32)
    mid = r.astype(BF16)
    lo = (r - mid.astype(F32)).astype(BF16)
    return hi, mid, lo


def _top3_mask(gate, pos, axis):
    sel = jnp.zeros(gate.shape, F32)
    for _ in range(MOBA_TOPK):
        m = jnp.max(gate, axis=axis, keepdims=True)
        idx = jnp.min(jnp.where(gate == m, pos, gate.shape[axis]), axis=axis, keepdims=True)
        hit = pos == idx
        sel = jnp.where(hit, 1.0, sel)
        gate = jnp.where(hit, -jnp.inf, gate)
    return sel


def _moba_prompt_kernel(qt_ref, k_ref, vt_ref, km_ref, bias_ref, o_ref,
                        qw_ref, s0_ref, s1_ref, p0_ref, p1_ref, a0_ref, a1_ref, m_ref, l_ref, acc_ref, sel_ref):
    i = pl.program_id(2)
    blk = MOBA_BLOCK
    dot = lambda a, b: jnp.dot(a, b, preferred_element_type=F32)
    qw_ref[...] = _masked_queries(qt_ref[0], HEAD_DIM, 2)
    qw = qw_ref[...]
    blk_id = lax.broadcasted_iota(jnp.int32, sel_ref.shape, 0)
    km_hi, km_mid, km_lo = _split3(km_ref[0])
    gate = dot(km_hi, qw) + dot(km_mid, qw) + dot(km_lo, qw)
    gate = jnp.where(blk_id < i, gate, -jnp.inf)
    sel_ref[...] = jnp.where(blk_id < i, _top3_mask(gate, blk_id, 0), jnp.where(blk_id == i, 1.0, 0.0))
    bias = lambda d: jnp.concatenate([bias_ref[0, 0, d], bias_ref[0, 1, d]], axis=1)
    mask = lambda s, n: jnp.where(sel_ref[pl.ds(n, 1), :] > 0.5, s, NEG)
    _attend_blocks(i, k_ref, vt_ref, qw_ref, (s0_ref, s1_ref), (p0_ref, p1_ref), (a0_ref, a1_ref),
                   m_ref, l_ref, acc_ref, bias, mask, defer_values=True)
    o = acc_ref[...] / l_ref[...]
    o_ref[0] = jnp.concatenate([o[:, :blk], o[:, blk:]], axis=0).T.astype(BF16)


def _prompt_attn_specs(S, nbp, off):
    blk = MOBA_BLOCK
    return dict(
        qt=pl.BlockSpec((1, LANES, blk), lambda b, p, i: (b, p + off, i)),
        k=pl.BlockSpec((1, S, LANES), lambda b, p, i: (b, 0, p + off)),
        vt=pl.BlockSpec((1, S // blk, LANES, blk), lambda b, p, i: (b, 0, p + off, 0)),
        km=pl.BlockSpec((1, nbp, LANES), lambda b, p, i: (b, 0, p + off)),
        bias=pl.BlockSpec((1, 2, N_BIAS_TILES, blk, blk), lambda b, p, i: (p + off, 0, 0, 0, 0)),
        out=pl.BlockSpec((1, blk, LANES), lambda b, p, i: (b, i, p)),
    )


def _moba_prompt(qt, k, vtb, kmean, bias_tiles):
    B, S, _ = k.shape
    blk = MOBA_BLOCK
    nbp = kmean.shape[1]
    sp = _prompt_attn_specs(S, nbp, 0)
    return pl.pallas_call(
        _moba_prompt_kernel,
        grid=(B, MOBA_HEADS // 2, S // blk),
        in_specs=[sp["qt"], sp["k"], sp["vt"], sp["km"], sp["bias"]],
        out_specs=sp["out"],
        out_shape=jax.ShapeDtypeStruct((B, S, MOBA_W), BF16),
        scratch_shapes=[pltpu.VMEM((LANES, 2 * blk), BF16),
                        pltpu.VMEM((blk, 2 * blk), F32), pltpu.VMEM((blk, 2 * blk), F32),
                        pltpu.VMEM((blk, 2 * blk), BF16), pltpu.VMEM((blk, 2 * blk), BF16),
                        pltpu.VMEM((1, 2 * blk), F32), pltpu.VMEM((1, 2 * blk), F32),
                        pltpu.VMEM((1, 2 * blk), F32), pltpu.VMEM((1, 2 * blk), F32),
                        pltpu.VMEM((HEAD_DIM, 2 * blk), F32), pltpu.VMEM((nbp, 2 * blk), F32)],
        compiler_params=_cparams(("parallel", "parallel", "arbitrary")),
        name="moba_prompt",
    )(qt, k, vtb, kmean, bias_tiles)


def _diff_prompt_kernel(lam_ref, qt_ref, k_ref, vt_ref, bias_ref, g_ref, o_ref,
                        qw_ref, s0_ref, s1_ref, p0_ref, p1_ref, a0_ref, a1_ref, m_ref, l_ref, acc_ref, *, out_scale):
    i = pl.program_id(2)
    blk = MOBA_BLOCK
    qw_ref[...] = _masked_queries(qt_ref[0], DIFF_QK, 4)
    bias = lambda d: jnp.concatenate([bias_ref[0, j // 2, d] for j in range(4)], axis=1)
    _attend_blocks(i, k_ref, vt_ref, qw_ref, (s0_ref, s1_ref), (p0_ref, p1_ref), (a0_ref, a1_ref),
                   m_ref, l_ref, acc_ref, bias, lambda s, n: s, defer_values=False)
    lam = lam_ref[0, 0]
    o = acc_ref[...] / l_ref[...]
    outs = []
    for h in range(2):
        oh = o[:, (2 * h) * blk:(2 * h + 1) * blk] - lam * o[:, (2 * h + 1) * blk:(2 * h + 2) * blk]
        outs.append(oh * lax.rsqrt(jnp.mean(oh * oh, axis=0, keepdims=True) + RMS_EPS))
    ot = jnp.concatenate(outs, axis=0) * g_ref[0] * out_scale
    o_ref[0] = ot.T.astype(BF16)


def _diff_prompt(qt, k, vtb, bias_tiles, lam, g_col, out_scale):
    B, S, _ = k.shape
    blk = MOBA_BLOCK
    sp = _prompt_attn_specs(S, 0, MOBA_HEADS // 2)
    return pl.pallas_call(
        functools.partial(_diff_prompt_kernel, out_scale=out_scale),
        grid=(B, DIFF_HEADS // 2, S // blk),
        in_specs=[pl.BlockSpec(memory_space=pltpu.SMEM), sp["qt"], sp["k"], sp["vt"], sp["bias"],
                  pl.BlockSpec((1, LANES, 1), lambda b, p, i: (p, 0, 0))],
        out_specs=sp["out"],
        out_shape=jax.ShapeDtypeStruct((B, S, DIFF_W), BF16),
        scratch_shapes=[pltpu.VMEM((LANES, 4 * blk), BF16),
                        pltpu.VMEM((blk, 4 * blk), F32), pltpu.VMEM((blk, 4 * blk), F32),
                        pltpu.VMEM((blk, 4 * blk), BF16), pltpu.VMEM((blk, 4 * blk), BF16),
                        pltpu.VMEM((1, 4 * blk), F32), pltpu.VMEM((1, 4 * blk), F32),
                        pltpu.VMEM((1, 4 * blk), F32), pltpu.VMEM((1, 4 * blk), F32),
                        pltpu.VMEM((HEAD_DIM, 4 * blk), F32)],
        compiler_params=_cparams(("parallel", "parallel", "arbitrary")),
        name="diff_prompt",
    )(lam, qt, k, vtb, bias_tiles, g_col)


def _post_a_kernel(x_ref, om_ref, od_ref, og_ref, wo_ref, g_ref, *rest):
    h = x_ref[...]
    h = h + jnp.dot(om_ref[...], wo_ref[0:MOBA_W, :], preferred_element_type=F32)
    h = h + jnp.dot(od_ref[...], wo_ref[MOBA_W:QKV_W, :], preferred_element_type=F32)
    h = h + jnp.dot(og_ref[...], wo_ref[QKV_W:, :], preferred_element_type=F32)
    if len(rest) == 2:
        h_ref, hn_ref = rest
    else:
        wr_ref, h_ref, hn_ref, route_ref = rest
    h_ref[...] = h
    hn = _rms(h, g_ref[...])
    hn_ref[...] = hn.astype(BF16)
    if len(rest) == 2:
        return
    a_hi, a_mid, _ = _split3(hn)
    dot = lambda a, b: jnp.dot(a, b, preferred_element_type=F32)
    logits = dot(a_hi, wr_ref[0]) + (dot(a_hi, wr_ref[1]) + dot(a_mid, wr_ref[0]))
    col = lax.broadcasted_iota(jnp.int32, logits.shape, 1)
    logits = jnp.where(col < N_EXPERTS, logits, -jnp.inf)
    m1 = jnp.max(logits, axis=1, keepdims=True)
    i1 = jnp.min(jnp.where(logits == m1, col, LANES), axis=1, keepdims=True)
    rest = jnp.where(col == i1, -jnp.inf, logits)
    m2 = jnp.max(rest, axis=1, keepdims=True)
    i2 = jnp.min(jnp.where(rest == m2, col, LANES), axis=1, keepdims=True)
    e = jnp.exp(m2 - m1)
    w1 = 1.0 / (1.0 + e)
    w2 = e / (1.0 + e)
    route = jnp.where(col == 0, i1.astype(F32), 0.0)
    route = jnp.where(col == 1, i2.astype(F32), route)
    route = jnp.where(col == 2, w1, route)
    route = jnp.where(col == 3, w2, route)
    route_ref[...] = route


def _post_a(x, om, od, og, w_out, g, w_router3, *, tm):
    T = x.shape[0]
    row = lambda w: pl.BlockSpec((tm, w), lambda i: (i, 0))
    full = lambda a: pl.BlockSpec(a.shape, lambda i: (0,) * a.ndim)
    routed = w_router3 is not None
    outs = pl.pallas_call(
        _post_a_kernel,
        grid=(T // tm,),
        in_specs=[row(D_MODEL), row(MOBA_W), row(DIFF_W), row(GMLP_W), full(w_out), full(g)]
                 + ([full(w_router3)] if routed else []),
        out_specs=(row(D_MODEL), row(D_MODEL)) + ((row(LANES),) if routed else ()),
        out_shape=(jax.ShapeDtypeStruct((T, D_MODEL), F32), jax.ShapeDtypeStruct((T, D_MODEL), BF16))
                  + ((jax.ShapeDtypeStruct((T, LANES), F32),) if routed else ()),
        compiler_params=_cparams(("parallel",)),
        name="post_a",
    )(x, om, od, og, w_out, g, *([w_router3] if routed else []))
    return outs if routed else (*outs, None)


def _ffn_kernel(te_ref, nu_ref, x_ref, wg_ref, wu_ref, wd_ref, sc_ref, o_ref, acc_ref):
    t = pl.program_id(0)
    f = pl.program_id(1)

    @pl.when(t < nu_ref[0])
    def _():
        x = x_ref[...]
        g = jnp.dot(x, wg_ref[0], preferred_element_type=F32)
        u = jnp.dot(x, wu_ref[0], preferred_element_type=F32)
        a = (g * jax.nn.sigmoid(g) * u).astype(BF16)
        part = jnp.dot(a, wd_ref[0], preferred_element_type=F32)

        @pl.when(f == 0)
        def _():
            acc_ref[...] = part

        @pl.when(f == 1)
        def _():
            o_ref[...] = (acc_ref[...] + part) * sc_ref[...]

    @pl.when(jnp.logical_and(t >= nu_ref[0], f == 1))
    def _():
        o_ref[...] = jnp.zeros_like(o_ref)


def _ffn(tile_expert, n_used, x, wg, wu, wd, scale, *, tm):
    N = x.shape[0]
    n_tiles = N // tm
    chunk = lambda t, f: jnp.where(t % 2 == 0, f, 1 - f)
    return pl.pallas_call(
        _ffn_kernel,
        grid_spec=pltpu.PrefetchScalarGridSpec(
            num_scalar_prefetch=2,
            grid=(n_tiles, 2),
            in_specs=[
                pl.BlockSpec((tm, D_MODEL), lambda t, f, te, nu: (t, 0)),
                pl.BlockSpec((1, D_MODEL, FF_CHUNK), lambda t, f, te, nu: (te[t], 0, chunk(t, f))),
                pl.BlockSpec((1, D_MODEL, FF_CHUNK), lambda t, f, te, nu: (te[t], 0, chunk(t, f))),
                pl.BlockSpec((1, FF_CHUNK, D_MODEL), lambda t, f, te, nu: (te[t], chunk(t, f), 0)),
                pl.BlockSpec((tm, 1), lambda t, f, te, nu: (t, 0)),
            ],
            out_specs=pl.BlockSpec((tm, D_MODEL), lambda t, f, te, nu: (t, 0)),
            scratch_shapes=[pltpu.VMEM((tm, D_MODEL), F32)],
        ),
        out_shape=jax.ShapeDtypeStruct((N, D_MODEL), F32),
        compiler_params=_cparams(("arbitrary", "arbitrary")),
        name="ffn",
    )(tile_expert, n_used, x, wg, wu, wd, scale)


def _moe_dispatch(route, *, tm):
    T = route.shape[0]
    n_assign = T * TOP_K
    n_tiles = -(-n_assign // tm) + N_EXPERTS
    experts = route[:, :TOP_K].astype(jnp.int32).reshape(n_assign)
    weights = route[:, TOP_K:2 * TOP_K].reshape(n_assign)
    onehot = (experts[:, None] == jnp.arange(N_EXPERTS)[None, :]).astype(jnp.int32)
    rank = jnp.sum((jnp.cumsum(onehot, axis=0) - onehot) * onehot, axis=1)
    counts = jnp.sum(onehot, axis=0)
    tiles_per = (counts + tm - 1) // tm
    tile_end = jnp.cumsum(tiles_per)
    slot = ((tile_end - tiles_per) * tm)[experts] + rank
    tile_expert = jnp.minimum(jnp.searchsorted(tile_end, jnp.arange(n_tiles), side="right"),
                              N_EXPERTS - 1).astype(jnp.int32)
    n_used = tile_end[-1:].astype(jnp.int32)
    token = (jnp.arange(n_assign, dtype=jnp.int32) // TOP_K).astype(F32)
    table = jnp.zeros((n_tiles * tm, 2), F32).at[slot].set(
        jnp.stack([token, weights], axis=1), unique_indices=True, mode="promise_in_bounds")
    return tile_expert, n_used, table[:, 0].astype(jnp.int32), table[:, 1:], slot.reshape(T, TOP_K)


def _post_b_kernel(*refs, n_y, final):
    h_ref = refs[0]
    y_refs = refs[1:1 + n_y]
    p_ref, gp_ref, wgate_ref, wproj_ref, gf_ref, o_ref = refs[1 + n_y:]
    h = h_ref[...]
    for y_ref in y_refs:
        h = h + y_ref[...]
    gate = jax.nn.sigmoid(jnp.dot(_rms(h, gp_ref[...]).astype(BF16), wgate_ref[...], preferred_element_type=F32))
    h = h + gate * jnp.dot(p_ref[...].astype(BF16), wproj_ref[...], preferred_element_type=F32)
    o_ref[...] = _rms(h, gf_ref[...]) if final else h


def _post_b(h, ys, p, g_ple, w_gate, w_proj, g_final, *, tm, final):
    T = h.shape[0]
    row = lambda w: pl.BlockSpec((tm, w), lambda i: (i, 0))
    full = lambda a: pl.BlockSpec(a.shape, lambda i: (0,) * a.ndim)
    return pl.pallas_call(
        functools.partial(_post_b_kernel, n_y=len(ys), final=final),
        grid=(T // tm,),
        in_specs=[row(D_MODEL)] + [row(D_MODEL)] * len(ys)
                 + [row(D_PLE), full(g_ple), full(w_gate), full(w_proj), full(g_final)],
        out_specs=row(D_MODEL),
        out_shape=jax.ShapeDtypeStruct((T, D_MODEL), F32),
        compiler_params=_cparams(("parallel",)),
        name="post_b",
    )(h, *ys, p, g_ple, w_gate, w_proj, g_final)


PAGES_PER_STEP = 16
BIAS_TAIL_BLOCKS = 8
DIFF_MAPS = 2 * DIFF_HEADS


def _sample_attn_kernel(pt_ref, lam_ref, *refs, n_tok, n_steps, n_tail, P, out_scale):
    k_refs = refs[:P]
    v_refs = refs[P:2 * P]
    (q_ref, kn_ref, vn_ref, bm_ref, bd_ref, bmf_ref, bdf_ref, bmo_ref, bdo_ref, g_ref, o_ref,
     qm_ref, qd_ref, gate_ref, mb_ref, lb_ref, accb_ref, m_ref, l_ref, acc_ref) = refs[2 * P:]
    s = pl.program_id(1)
    rm = MOBA_HEADS * n_tok
    rd = DIFF_MAPS * n_tok
    lane = lax.broadcasted_iota(jnp.int32, (1, LANES), 1)

    @pl.when(s == 0)
    def _():
        q = q_ref[0]
        qm = jnp.concatenate([q[:, :MOBA_W]] * MOBA_HEADS, axis=0)
        r = lax.broadcasted_iota(jnp.int32, qm.shape, 0) // n_tok
        c = lax.broadcasted_iota(jnp.int32, qm.shape, 1) // HEAD_DIM
        qm_ref[...] = jnp.where(r == c, qm, 0.0).astype(BF16)
        qd = jnp.concatenate([q[:, MOBA_W:]] * DIFF_MAPS, axis=0)
        r = lax.broadcasted_iota(jnp.int32, qd.shape, 0) // n_tok
        c = lax.broadcasted_iota(jnp.int32, qd.shape, 1) // DIFF_QK
        qd_ref[...] = jnp.where(r == c, qd, 0.0).astype(BF16)
        gate_ref[...] = jnp.full(gate_ref.shape, -jnp.inf, F32)
        mb_ref[...] = jnp.full(mb_ref.shape, NEG, F32)
        lb_ref[...] = jnp.zeros(lb_ref.shape, F32)
        m_ref[...] = jnp.full(m_ref.shape, NEG, F32)
        l_ref[...] = jnp.zeros(l_ref.shape, F32)
        acc_ref[...] = jnp.zeros(acc_ref.shape, F32)

    def scores(q, k, transposed):
        return jnp.dot(q, k, preferred_element_type=F32) if transposed else _qk(q, k)

    def weighted(p, v, transposed):
        return _qk(p, v) if transposed else jnp.dot(p, v, preferred_element_type=F32)

    def diff_update(kd, vd, bias, transposed):
        sd = scores(qd_ref[...], kd, transposed) + bias
        m_old = m_ref[...]
        m = jnp.maximum(m_old, jnp.max(sd, axis=1, keepdims=True))
        alpha = jnp.exp(m_old - m)
        p = jnp.exp(sd - m)
        m_ref[...] = m
        l_ref[...] = alpha * l_ref[...] + jnp.sum(p, axis=1, keepdims=True)
        acc_ref[...] = alpha * acc_ref[...] + weighted(p.astype(BF16), vd, transposed)

    def moba_partial(km, vm, bias, transposed):
        raw = scores(qm_ref[...], km, transposed)
        sm = raw + bias
        m = jnp.max(sm, axis=1, keepdims=True)
        p = jnp.exp(sm - m)
        return raw, m, jnp.sum(p, axis=1, keepdims=True), weighted(p.astype(BF16), vm, transposed)

    page = lambda ref: ref[0, 0].reshape(QKV_W, PAGE_SIZE)
    kt = jnp.concatenate([page(r) for r in k_refs], axis=1).astype(BF16)
    vt = jnp.concatenate([page(r) for r in v_refs], axis=1).astype(BF16)
    blk = MOBA_BLOCK
    nb = n_steps * (P // 2)

    def bias_of(n, tail_ref, far_ref):
        t = n - (nb - n_tail)
        return jnp.where(t >= 0, tail_ref[jnp.maximum(t, 0)], far_ref[...])

    raw = jnp.dot(qm_ref[...], kt[:MOBA_W], preferred_element_type=F32)
    gate, mb, lb = gate_ref[...], mb_ref[...], lb_ref[...]
    for bi in range(P // 2):
        n = s * (P // 2) + bi
        cols = slice(bi * blk, (bi + 1) * blk)
        sm = raw[:, cols] + bias_of(n, bm_ref, bmf_ref)
        m = jnp.max(sm, axis=1, keepdims=True)
        p = jnp.exp(sm - m)
        accb_ref[n] = _qk(p.astype(BF16), vt[:MOBA_W, cols])
        gate = jnp.where(lane == n, jnp.mean(raw[:, cols], axis=1, keepdims=True), gate)
        mb = jnp.where(lane == n, m, mb)
        lb = jnp.where(lane == n, jnp.sum(p, axis=1, keepdims=True), lb)
    gate_ref[...], mb_ref[...], lb_ref[...] = gate, mb, lb
    bias_d = jnp.concatenate([bias_of(s * (P // 2) + bi, bd_ref, bdf_ref) for bi in range(P // 2)], axis=1)
    diff_update(kt[MOBA_W:], vt[MOBA_W:], bias_d, True)

    @pl.when(s == n_steps - 1)
    def _():
        kn = kn_ref[0].astype(BF16)
        vn = vn_ref[0].astype(BF16)
        diff_update(kn[:, MOBA_W:], vn[:, MOBA_W:], bdo_ref[...], False)
        _, m_own, l_own, acc_own = moba_partial(kn[:, :MOBA_W], vn[:, :MOBA_W], bmo_ref[...], False)
        col = lax.broadcasted_iota(jnp.int32, (rm, LANES), 1)
        sel = _top3_mask(gate_ref[...], col, 1)
        sel = jnp.where(gate_ref[...] > -jnp.inf, sel, 0.0)
        mb = mb_ref[...]
        m_fin = jnp.maximum(m_own, jnp.max(jnp.where(sel > 0.5, mb, NEG), axis=1, keepdims=True))
        w = sel * jnp.exp(jnp.minimum(mb - m_fin, 0.0))
        w_own = jnp.exp(m_own - m_fin)
        l_fin = w_own * l_own + jnp.sum(w * lb_ref[...], axis=1, keepdims=True)

        def merge(nb, acc):
            wn = jnp.sum(jnp.where(col == nb, w, 0.0), axis=1, keepdims=True)
            return acc + wn * accb_ref[nb]

        acc_fin = lax.fori_loop(0, n_steps * (P // 2), merge, w_own * acc_own)
        om_rows = acc_fin / l_fin
        lane_m = lax.broadcasted_iota(jnp.int32, (n_tok, MOBA_W), 1) // HEAD_DIM
        om = jnp.zeros((n_tok, MOBA_W), F32)
        for h in range(MOBA_HEADS):
            om = jnp.where(lane_m == h, om_rows[h * n_tok:(h + 1) * n_tok], om)

        od_rows = acc_ref[...] / l_ref[...]
        lane_d = lax.broadcasted_iota(jnp.int32, (n_tok, DIFF_W), 1) // HEAD_DIM
        lam = lam_ref[0, 0]
        od = jnp.zeros((n_tok, DIFF_W), F32)
        for h in range(DIFF_HEADS):
            a1 = od_rows[(2 * h) * n_tok:(2 * h + 1) * n_tok]
            a2 = od_rows[(2 * h + 1) * n_tok:(2 * h + 2) * n_tok]
            od = jnp.where(lane_d == h, a1 - lam * a2, od)
        normed = jnp.zeros_like(od)
        for h in range(DIFF_HEADS):
            ms = jnp.sum(jnp.where(lane_d == h, od * od, 0.0), axis=1, keepdims=True) * (1.0 / HEAD_DIM)
            normed = jnp.where(lane_d == h, od * lax.rsqrt(ms + RMS_EPS), normed)
        o_ref[0, :, :MOBA_W] = om
        o_ref[0, :, MOBA_W:] = normed * g_ref[...] * out_scale


def _sample_attn(layer, page_table, cache_kt, cache_vt, q, k_new, v_new,
                 bias_m, bias_d, bias_mf, bias_df, bias_mo, bias_do, lam, g, out_scale):
    nseq, n_pages = page_table.shape
    n_tok = q.shape[1]
    P = PAGES_PER_STEP if n_pages % PAGES_PER_STEP == 0 else PAGES_PER_STEP // 2
    assert n_pages % P == 0 and n_pages // 2 <= LANES and n_tok % 8 == 0
    n_steps = n_pages // P
    nb = n_pages // 2
    n_tail = bias_m.shape[0]
    rm, rd = MOBA_HEADS * n_tok, DIFF_MAPS * n_tok
    page = lambda j: pl.BlockSpec((1, 1, KV_HEADS, HEAD_DIM, PAGE_SIZE),
                                  lambda b, s, pt, j=j: (layer, pt[b, s * P + j], 0, 0, 0))
    const = lambda a: pl.BlockSpec(a.shape, lambda b, s, pt: (0,) * a.ndim)
    per_seq = lambda a: pl.BlockSpec((1,) + a.shape[1:], lambda b, s, pt: (b,) + (0,) * (a.ndim - 1))
    return pl.pallas_call(
        functools.partial(_sample_attn_kernel, n_tok=n_tok, n_steps=n_steps, n_tail=n_tail, P=P,
                          out_scale=out_scale),
        grid_spec=pltpu.PrefetchScalarGridSpec(
            num_scalar_prefetch=1,
            grid=(nseq, n_steps),
            in_specs=[pl.BlockSpec(memory_space=pltpu.SMEM)]
                     + [page(j) for j in range(P)] + [page(j) for j in range(P)]
                     + [per_seq(q), per_seq(k_new), per_seq(v_new),
                        const(bias_m), const(bias_d), const(bias_mf), const(bias_df), const(bias_mo), const(bias_do),
                        const(g)],
            out_specs=pl.BlockSpec((1, n_tok, QKV_W), lambda b, s, pt: (b, 0, 0)),
            scratch_shapes=[
                pltpu.VMEM((rm, MOBA_W), BF16), pltpu.VMEM((rd, DIFF_W), BF16),
                pltpu.VMEM((rm, LANES), F32), pltpu.VMEM((rm, LANES), F32), pltpu.VMEM((rm, LANES), F32),
                pltpu.VMEM((nb, rm, MOBA_W), F32),
                pltpu.VMEM((rd, 1), F32), pltpu.VMEM((rd, 1), F32), pltpu.VMEM((rd, DIFF_W), F32),
            ],
        ),
        out_shape=jax.ShapeDtypeStruct((nseq, n_tok, QKV_W), F32),
        compiler_params=_cparams(("parallel", "arbitrary")),
        name="sample_attn",
    )(page_table, lam, *([cache_kt] * P), *([cache_vt] * P), q, k_new, v_new,
      bias_m, bias_d, bias_mf, bias_df, bias_mo, bias_do, g)


def _sample_bias_kernel(rowtab_ref, o_ref, *, first_pos, past_len, n_tok):
    n = pl.program_id(0)
    rows, blk = o_ref.shape[1], o_ref.shape[2]
    t = lax.broadcasted_iota(jnp.int32, (rows, blk), 0) % n_tok
    pos = first_pos + n * blk + lax.broadcasted_iota(jnp.int32, (rows, blk), 1)
    o_ref[0] = _bias_of_distance(past_len + t - pos, lambda b: rowtab_ref[:, b:b + 1])


def _sample_bias(bias_table, past_len, n_tok):
    tab = bias_table.astype(F32).T
    tab_m = jnp.repeat(tab[:MOBA_HEADS], n_tok, axis=0)
    tab_d = jnp.repeat(tab[MOBA_HEADS:], 2 * n_tok, axis=0)
    nb = past_len // MOBA_BLOCK
    n_tail = min(nb, BIAS_TAIL_BLOCKS)
    assert n_tail == nb or n_tail * MOBA_BLOCK + 1 >= _bucket_thresholds()[-1]

    def past(rowtab):
        rows = rowtab.shape[0]
        return pl.pallas_call(
            functools.partial(_sample_bias_kernel, first_pos=(nb - n_tail) * MOBA_BLOCK, past_len=past_len,
                              n_tok=n_tok),
            grid=(n_tail,),
            in_specs=[pl.BlockSpec(rowtab.shape, lambda n: (0, 0))],
            out_specs=pl.BlockSpec((1, rows, MOBA_BLOCK), lambda n: (n, 0, 0)),
            out_shape=jax.ShapeDtypeStruct((n_tail, rows, MOBA_BLOCK), F32),
            compiler_params=_cparams(("parallel",)),
            name="sample_bias",
        )(rowtab)

    far = lambda rowtab: jnp.broadcast_to(rowtab[:, NUM_BUCKETS - 1:], (rowtab.shape[0], MOBA_BLOCK))

    t = np.arange(n_tok)
    d_own = t[:, None] - np.arange(LANES)[None, :]
    valid = (d_own >= 0) & (np.arange(LANES)[None, :] < n_tok)

    def own(rowtab):
        reps = rowtab.shape[0] // n_tok
        picked = jnp.take_along_axis(rowtab, jnp.asarray(np.tile(_bucket_np(d_own), (reps, 1))), axis=1)
        return jnp.where(jnp.asarray(np.tile(valid, (reps, 1))), picked, NEG)

    return past(tab_m), past(tab_d), far(tab_m), far(tab_d), own(tab_m), own(tab_d)


def _lambda_init(i):
    return 0.8 - 0.6 * math.exp(-0.3 * i)


def _perm_w_in(w):
    sizes = (MOBA_W, MOBA_W, MOBA_W, DIFF_W, DIFF_W, DIFF_W, GMLP_W, GMLP_W)
    o = np.concatenate([[0], np.cumsum(sizes)])
    qm, km, vm, qd, kd, vd, u, vg = [w[:, o[j]:o[j + 1]] for j in range(8)]
    return jnp.concatenate([qm, qd, km, kd, vm, vd, u, vg], axis=1).astype(BF16)


def _pad_router(w):
    wp = jnp.pad(w.astype(F32), ((0, 0), (0, LANES - N_EXPERTS)))
    return jnp.stack(_split3(wp))


TM_PROMPT, TM_SAMPLE = 512, 256
TMF_PROMPT, TMF_SAMPLE = 512, 128


def _trunk(h, p_layers, mix_fn, W, *, tm, tm_f):
    T = h.shape[0]
    depth = len(W)
    ks, vs, gs = [], [], []
    for i, lw in enumerate(W):
        om, od, og, k_out, v_out, vg = mix_fn(i, lw, h)
        h1, hn, route = _post_a(h, om, od, og, lw["w_out"], lw["g_ffn"], lw["w_router"], tm=tm)
        if lw["moe"]:
            te, nu, slot_token, slot_scale, slot = _moe_dispatch(route, tm=tm_f)
            rows = lambda a, idx: a.at[idx].get(mode="promise_in_bounds")
            y = _ffn(te, nu, rows(hn, slot_token), lw["wg"], lw["wu"], lw["wd"], slot_scale, tm=tm_f)
            ys = [rows(y, slot[:, j]) for j in range(TOP_K)]
        else:
            n_tiles = T // tm_f
            ys = [_ffn(jnp.zeros((n_tiles,), jnp.int32), jnp.full((1,), n_tiles, jnp.int32), hn,
                       lw["wg"], lw["wu"], lw["wd"], jnp.ones((T, 1), F32), tm=tm_f)]
        h = _post_b(h1, ys, p_layers[i].reshape(T, D_PLE), lw["g_ple"], lw["w_ple_gate"], lw["w_ple_proj"],
                    lw["g_final"], tm=tm, final=(i == depth - 1))
        ks.append(k_out)
        vs.append(v_out)
        gs.append(vg)
    return h, ks, vs, gs


def kernel(x_prompt, x_sample, cache_k, cache_v, page_table, p_prompt, p_sample, bias_table, norm_mix, w_in, w_out, lambda_q1, lambda_k1, lambda_q2, lambda_k2, diff_norm, gmlp_w_s, gmlp_b_s, norm_ffn, w_ffn_gate, w_ffn_up, w_ffn_down, w_router, w_exp_gate, w_exp_up, w_exp_down, norm_ple, w_ple_gate, w_ple_proj, norm_final):
    B, S, _ = x_prompt.shape
    nseq, n_tok, _ = x_sample.shape
    depth = cache_k.shape[0]
    n_pages = page_table.shape[1]
    past_len = n_pages * PAGE_SIZE
    nblk = S // MOBA_BLOCK
    assert S % TM_PROMPT == 0 and nblk <= LANES and past_len % MOBA_BLOCK == 0
    assert GMLP_CHUNK % n_tok == 0 and (nseq * n_tok) % TM_SAMPLE == 0

    row2 = lambda a: a.astype(F32).reshape(1, -1)
    q_scale = jnp.asarray(np.concatenate([np.full(MOBA_W, HEAD_DIM ** -0.5, np.float32),
                                          np.full(DIFF_W, DIFF_QK ** -0.5, np.float32)])[None])
    tril = jnp.asarray(np.tril(np.ones((GMLP_CHUNK, GMLP_CHUNK), np.float32)))
    reps = GMLP_CHUNK // n_tok
    W = []
    for i in range(depth):
        ws = gmlp_w_s[i].astype(F32) * tril
        bs = gmlp_b_s[i].astype(F32)
        mix_prompt = (ws.astype(BF16), jnp.repeat(bs.T, HEAD_DIM, axis=1))
        w_small = ws[:, :n_tok, :n_tok]
        w_blockdiag = jnp.einsum("ab,gts->gatbs", jnp.eye(reps, dtype=F32), w_small).reshape(
            GMLP_GROUPS, GMLP_CHUNK, GMLP_CHUNK)
        mix_sample = (w_blockdiag.astype(BF16), jnp.tile(jnp.repeat(bs[:, :n_tok].T, HEAD_DIM, axis=1), (reps, 1)))
        lam = (jnp.exp(jnp.sum(lambda_q1[i].astype(F32) * lambda_k1[i].astype(F32)))
               - jnp.exp(jnp.sum(lambda_q2[i].astype(F32) * lambda_k2[i].astype(F32))) + _lambda_init(i))
        moe = i % 2 == 1
        j = i // 2
        w_perm = _perm_w_in(w_in[i])
        W.append(dict(
            g_mix=row2(norm_mix[i]), w_in=w_perm,
            w_in_t=w_perm[:, :3 * QKV_W].T.reshape(3, QKV_W, D_MODEL),
            w_in_kuv=jnp.concatenate([w_perm[:, QKV_W:2 * QKV_W], w_perm[:, 3 * QKV_W:]], axis=1),
            mix_prompt=mix_prompt, mix_sample=mix_sample,
            w_out=w_out[i].astype(BF16), g_ffn=row2(norm_ffn[i]),
            w_router=_pad_router(w_router[j]) if moe else None, moe=moe,
            wg=(w_exp_gate[j] if moe else w_ffn_gate[j][None]).astype(BF16),
            wu=(w_exp_up[j] if moe else w_ffn_up[j][None]).astype(BF16),
            wd=(w_exp_down[j] if moe else w_ffn_down[j][None]).astype(BF16),
            g_ple=row2(norm_ple[i]), w_ple_gate=w_ple_gate[i].astype(BF16), w_ple_proj=w_ple_proj[i].astype(BF16),
            g_final=row2(norm_final), lam=lam.reshape(1, 1).astype(F32),
            g_diff=diff_norm[i].astype(F32), out_scale=1.0 - _lambda_init(i),
        ))

    log2e = math.log2(math.e)
    bias_tiles = _prompt_bias_tiles(bias_table.astype(F32) * log2e).reshape(
        KV_HEADS // 2, 2, N_BIAS_TILES, MOBA_BLOCK, MOBA_BLOCK)
    q_scale_prompt = (q_scale * log2e).T
    nbp = -(-nblk // 16) * 16

    def prompt_mix(i, lw, h):
        qt, kb, kt, vt, vtb, og, km = _inproj_prompt(h, lw["g_mix"], lw["w_in_t"], lw["w_in_kuv"], q_scale_prompt,
                                                     *lw["mix_prompt"], B=B, S=S, tm=TM_PROMPT)
        k3 = kb.reshape(B, S, QKV_W)
        kmean = jnp.pad(km.reshape(B, nblk, QKV_W), ((0, 0), (0, nbp - nblk), (0, 0)))
        om = _moba_prompt(qt, k3, vtb, kmean, bias_tiles)
        od = _diff_prompt(qt, k3, vtb, bias_tiles, lw["lam"], lw["g_diff"].reshape(DIFF_HEADS // 2, LANES, 1),
                          lw["out_scale"])
        return om.reshape(B * S, MOBA_W), od.reshape(B * S, DIFF_W), og, kt, vt, None

    y_p, k_p, v_p, _ = _trunk(x_prompt.reshape(B * S, D_MODEL), p_prompt, prompt_mix, W,
                              tm=TM_PROMPT, tm_f=TMF_PROMPT)

    cache_kt = jnp.transpose(cache_k, (0, 1, 3, 4, 2))
    cache_vt = jnp.transpose(cache_v, (0, 1, 3, 4, 2))
    sbias = _sample_bias(bias_table, past_len, n_tok)

    def sample_mix(i, lw, h):
        q, kf, vf, og, vg = _inproj_sample(h, lw["g_mix"], lw["w_in"], q_scale, *lw["mix_sample"], tm=TM_SAMPLE)
        pad_new = lambda a: jnp.pad(a.reshape(nseq, n_tok, QKV_W), ((0, 0), (0, LANES - n_tok), (0, 0)))
        o = _sample_attn(i, page_table, cache_kt, cache_vt, q.reshape(nseq, n_tok, QKV_W),
                         pad_new(kf), pad_new(vf), *sbias, lw["lam"], lw["g_diff"].reshape(1, DIFF_W),
                         lw["out_scale"])
        o = o.reshape(nseq * n_tok, QKV_W).astype(BF16)
        return o[:, :MOBA_W], o[:, MOBA_W:], og, kf, vf, vg

    y_s, k_s, v_s, g_s = _trunk(x_sample.reshape(nseq * n_tok, D_MODEL), p_sample, sample_mix, W,
                                tm=TM_SAMPLE, tm_f=TMF_SAMPLE)

    heads_t = lambda cols: jnp.transpose(
        jnp.stack(cols).reshape(depth, B, KV_HEADS, HEAD_DIM, S), (0, 1, 4, 2, 3))
    heads = lambda rows: jnp.stack(rows).reshape(depth, nseq, n_tok, KV_HEADS, HEAD_DIM)
    return (y_p.reshape(B, S, D_MODEL), y_s.reshape(nseq, n_tok, D_MODEL),
            heads_t(k_p), heads_t(v_p), heads(k_s), heads(v_s),
            jnp.stack(g_s).reshape(depth, nseq, n_tok, GMLP_W))
```
